```python
import math
import jax, jax.numpy as jnp
from jax import lax
import numpy as np

D_MODEL = 1024
BATCH = 8
SEQ = 4096
DEPTH = 2

N_MIXERS = 2
SSM_GROUP = 16
SSM_GROUPS = D_MODEL // SSM_GROUP
SSM_STATE = 64
DT_MIN = 1e-3
DT_MAX = 1e-1
SSM_C_STD = 0.5
SB_HEADS = 16
SB_HEAD_DIM = D_MODEL // SB_HEADS
Q_BLOCK = 128
FFN_HIDDEN = -(-8 * D_MODEL // (3 * 256)) * 256
DEEPNORM_ALPHA = (2 * DEPTH) ** 0.25
DEEPNORM_BETA = (8 * DEPTH) ** -0.25
LN_EPS = 1e-5
N_SSM_LAYERS = (DEPTH + 1) // 2
N_SB_LAYERS = DEPTH // 2

kernel_name = "interleaved_s5_stickbreaking_deepnorm"


def layer_norm(x, g, b):
    xf = x.astype(jnp.float32)
    mu = jnp.mean(xf, axis=-1, keepdims=True)
    var = jnp.mean(jnp.square(xf - mu), axis=-1, keepdims=True)
    y = (xf - mu) * lax.rsqrt(var + LN_EPS) * g.astype(jnp.float32) + b.astype(jnp.float32)
    return y.astype(x.dtype)


def _cmul(ar, ai, br, bi):
    return ar * br - ai * bi, ar * bi + ai * br


def _ssm_combine(left, right):
    alr, ali, blr, bli = left
    arr, ari, brr, bri = right
    ar, ai = _cmul(arr, ari, alr, ali)
    br, bi = _cmul(arr, ari, blr, bli)
    return ar, ai, br + brr, bi + bri


def s5_mixer(x, a_re, a_im, log_dt, b_re, b_im, c_re, c_im, d_skip, w_glu, w_out):
    bsz, seq, _ = x.shape
    f32 = jnp.float32
    u = x.astype(f32).reshape(bsz, seq, SSM_GROUPS, SSM_GROUP)
    dt = jnp.exp(log_dt.astype(f32))[:, None]
    lam_re = jnp.minimum(a_re.astype(f32), -1e-4)
    lam_im = a_im.astype(f32)
    mag = jnp.exp(lam_re * dt)
    lbar_re = mag * jnp.cos(lam_im * dt)
    lbar_im = mag * jnp.sin(lam_im * dt)
    den = lam_re * lam_re + lam_im * lam_im
    num_re = lbar_re - 1.0
    num_im = lbar_im
    coef_re = (num_re * lam_re + num_im * lam_im) / den
    coef_im = (num_im * lam_re - num_re * lam_im) / den
    bb_re, bb_im = _cmul(coef_re[..., None], coef_im[..., None],
                         b_re.astype(f32), b_im.astype(f32))
    bu_re = jnp.einsum('blgh,gph->lbgp', u, bb_re)
    bu_im = jnp.einsum('blgh,gph->lbgp', u, bb_im)
    a_seq_re = jnp.broadcast_to(lbar_re, (seq, 1, SSM_GROUPS, SSM_STATE))
    a_seq_im = jnp.broadcast_to(lbar_im, (seq, 1, SSM_GROUPS, SSM_STATE))
    _, _, s_re, s_im = lax.associative_scan(
        _ssm_combine, (a_seq_re, a_seq_im, bu_re, bu_im), axis=0)
    y = (jnp.einsum('lbgp,ghp->blgh', s_re, c_re.astype(f32))
         - jnp.einsum('lbgp,ghp->blgh', s_im, c_im.astype(f32)))
    y = y.reshape(bsz, seq, D_MODEL) + d_skip.astype(f32) * x.astype(f32)
    h = jax.nn.gelu(y).astype(x.dtype)
    val, gate = jnp.split(h @ w_glu, 2, axis=-1)
    z = val * jax.nn.sigmoid(gate)
    return (z @ w_out).astype(x.dtype)


def stick_breaking_mixer(x, w_qkv, w_o):
    bsz, seq, _ = x.shape
    f32 = jnp.float32
    q, k, v = jnp.split(x @ w_qkv, 3, axis=-1)
    to_heads = lambda t: t.reshape(bsz, seq, SB_HEADS, SB_HEAD_DIM).transpose(0, 2, 1, 3)
    q, k, v = to_heads(q), to_heads(k), to_heads(v)
    scale = SB_HEAD_DIM ** -0.5
    outs = []
    for blk in range(seq // Q_BLOCK):
        q0 = blk * Q_BLOCK
        kv_len = q0 + Q_BLOCK
        qb = q[:, :, q0:kv_len].astype(f32)
        kb = k[:, :, :kv_len].astype(f32)
        vb = v[:, :, :kv_len]
        z = jnp.einsum('bhqd,bhkd->bhqk', qb, kb) * scale
        t_idx = q0 + jnp.arange(Q_BLOCK)[:, None]
        s_idx = jnp.arange(kv_len)[None, :]
        past = s_idx < t_idx
        log_beta = jax.nn.log_sigmoid(z)
        log_1m_beta = jnp.where(past, jax.nn.log_sigmoid(-z), 0.0)
        between = lax.cumsum(log_1m_beta, axis=3, reverse=True) - log_1m_beta
        weights = jnp.where(past, jnp.exp(log_beta + between), 0.0)
        outs.append(jnp.einsum('bhqk,bhkd->bhqd', weights.astype(vb.dtype), vb))
    o = jnp.concatenate(outs, axis=2)
    o = o.transpose(0, 2, 1, 3).reshape(bsz, seq, D_MODEL)
    return (o @ w_o).astype(x.dtype)


def swiglu_ffn(x, w_gu, w_down):
    g, u = jnp.split(x @ w_gu, 2, axis=-1)
    return (jax.nn.silu(g) * u) @ w_down


def setup_inputs(seed: int = 0) -> dict:
    key = jax.random.key(seed)
    ks = jax.random.split(key, 24)
    f32 = jnp.float32
    nrm = lambda k, shape, std: jax.random.normal(k, shape, f32) * std
    G, P, H, D, F = SSM_GROUPS, SSM_STATE, SSM_GROUP, D_MODEL, FFN_HIDDEN
    x = jax.random.normal(ks[0], (BATCH, SEQ, D), f32)
    ssm_a_re = -0.5 + nrm(ks[1], (N_SSM_LAYERS, G, P), 0.01)
    ssm_a_im = (math.pi * jnp.arange(P, dtype=f32))[None, None, :] + nrm(ks[2], (N_SSM_LAYERS, G, P), 0.01)
    ssm_log_dt = jax.random.uniform(ks[3], (N_SSM_LAYERS, G), f32,
                                    minval=math.log(DT_MIN), maxval=math.log(DT_MAX))
    ssm_b_re = nrm(ks[4], (N_SSM_LAYERS, G, P, H), (2 * H) ** -0.5)
    ssm_b_im = nrm(ks[5], (N_SSM_LAYERS, G, P, H), (2 * H) ** -0.5)
    ssm_c_re = nrm(ks[6], (N_SSM_LAYERS, G, H, P), SSM_C_STD)
    ssm_c_im = nrm(ks[7], (N_SSM_LAYERS, G, H, P), SSM_C_STD)
    ssm_d = nrm(ks[8], (N_SSM_LAYERS, D), 1.0)
    ssm_w_glu = nrm(ks[9], (N_SSM_LAYERS, D, 2 * D), D ** -0.5)
    ssm_w_out = nrm(ks[10], (N_SSM_LAYERS, D, D), D ** -0.5 * DEEPNORM_BETA)
    sb_w_qkv = nrm(ks[11], (N_SB_LAYERS, D, 3 * D), D ** -0.5)
    sb_w_o = nrm(ks[12], (N_SB_LAYERS, D, D), D ** -0.5 * DEEPNORM_BETA)
    ffn_w_gu = nrm(ks[13], (DEPTH, D, 2 * F), D ** -0.5)
    ffn_w_down = nrm(ks[14], (DEPTH, F, D), F ** -0.5 * DEEPNORM_BETA)
    ln_mix_g = 1.0 + nrm(ks[15], (DEPTH, D), 0.02)
    ln_mix_b = nrm(ks[16], (DEPTH, D), 0.02)
    ln_ffn_g = 1.0 + nrm(ks[17], (DEPTH, D), 0.02)
    ln_ffn_b = nrm(ks[18], (DEPTH, D), 0.02)
    return {"x": x,
            "ssm_a_re": ssm_a_re, "ssm_a_im": ssm_a_im, "ssm_log_dt": ssm_log_dt,
            "ssm_b_re": ssm_b_re, "ssm_b_im": ssm_b_im,
            "ssm_c_re": ssm_c_re, "ssm_c_im": ssm_c_im, "ssm_d": ssm_d,
            "ssm_w_glu": ssm_w_glu, "ssm_w_out": ssm_w_out,
            "sb_w_qkv": sb_w_qkv, "sb_w_o": sb_w_o,
            "ffn_w_gu": ffn_w_gu, "ffn_w_down": ffn_w_down,
            "ln_mix_g": ln_mix_g, "ln_mix_b": ln_mix_b,
            "ln_ffn_g": ln_ffn_g, "ln_ffn_b": ln_ffn_b}


def reference(x, ssm_a_re, ssm_a_im, ssm_log_dt, ssm_b_re, ssm_b_im, ssm_c_re, ssm_c_im,
              ssm_d, ssm_w_glu, ssm_w_out, sb_w_qkv, sb_w_o, ffn_w_gu, ffn_w_down,
              ln_mix_g, ln_mix_b, ln_ffn_g, ln_ffn_b):
    h = x
    for i in range(DEPTH):
        j = i // N_MIXERS
        if i % N_MIXERS == 0:
            mix = s5_mixer(h, ssm_a_re[j], ssm_a_im[j], ssm_log_dt[j], ssm_b_re[j], ssm_b_im[j],
                           ssm_c_re[j], ssm_c_im[j], ssm_d[j], ssm_w_glu[j], ssm_w_out[j])
        else:
            mix = stick_breaking_mixer(h, sb_w_qkv[j], sb_w_o[j])
        h = layer_norm(DEEPNORM_ALPHA * h + mix, ln_mix_g[i], ln_mix_b[i])
        h = layer_norm(DEEPNORM_ALPHA * h + swiglu_ffn(h, ffn_w_gu[i], ffn_w_down[i]),
                       ln_ffn_g[i], ln_ffn_b[i])
    return h
```

```python
import functools
import math

import jax
import jax.numpy as jnp
from jax import lax
from jax.experimental import pallas as pl
from jax.experimental.pallas import tpu as pltpu

F32 = jnp.float32
BF16 = jnp.bfloat16

V7X_LANES = 128
V7X_SUBLANES = 8
V7X_VMEM_LIMIT_BYTES = 56 * 1024 * 1024

SSM_GROUP = 16
HEAD_DIM = 64
LN_EPS = 1e-5
LAM_RE_MAX = -1e-4

S5_TIME_TILE = 128
ROW_TILE = 512
FFN_ROW_TILE = 256
ATTN_Q_TILE = 512
ATTN_K_TILE = 256


def _params(*sem):
    return pltpu.CompilerParams(dimension_semantics=sem,
                                vmem_limit_bytes=V7X_VMEM_LIMIT_BYTES)


def _resident(shape):
    zeros = (0,) * len(shape)
    return pl.BlockSpec(shape, lambda *_: zeros, pipeline_mode=pl.Buffered(1))


def _sigmoid(x):
    return 1.0 / (1.0 + jnp.exp(-x))


def _gelu_tanh(x):
    c = math.sqrt(2.0 / math.pi)
    return 0.5 * x * (1.0 + jnp.tanh(c * (x + 0.044715 * (x * x * x))))


def _layer_norm(r, g, b):
    mu = jnp.mean(r, axis=-1, keepdims=True)
    d = r - mu
    var = jnp.mean(d * d, axis=-1, keepdims=True)
    return d * lax.rsqrt(var + LN_EPS) * g + b


def _s5_kernel(x_ref, bm_ref, cm_ref, are_ref, aim_ref, d_ref, h_ref,
               state_ref, bu_ref, *, time_tile, batch):
    half = bu_ref.shape[1] // 2
    nvr = half // V7X_LANES

    @pl.when(pl.program_id(1) == 0)
    def _():
        state_ref[...] = jnp.zeros_like(state_ref)

    x = x_ref[...]
    bu_ref[...] = jnp.dot(x.astype(BF16), bm_ref[0], preferred_element_type=F32)

    def lanes(p, imag):
        lo = (half if imag else 0) + p * V7X_LANES
        return slice(lo, lo + V7X_LANES)

    a_re = [are_ref[0, :, lanes(p, False)] for p in range(nvr)]
    a_im = [aim_ref[0, :, lanes(p, False)] for p in range(nvr)]
    s_re0 = tuple(state_ref[:, lanes(p, False)] for p in range(nvr))
    s_im0 = tuple(state_ref[:, lanes(p, True)] for p in range(nvr))

    def body(t, carry):
        s_re, s_im = carry
        rows = pl.ds(pl.multiple_of(t * batch, batch), batch)
        n_re, n_im = [], []
        for p in range(nvr):
            b_re = bu_ref[rows, lanes(p, False)]
            b_im = bu_ref[rows, lanes(p, True)]
            v_re = a_re[p] * s_re[p] - a_im[p] * s_im[p] + b_re
            v_im = a_re[p] * s_im[p] + a_im[p] * s_re[p] + b_im
            bu_ref[rows, lanes(p, False)] = v_re
            bu_ref[rows, lanes(p, True)] = v_im
            n_re.append(v_re)
            n_im.append(v_im)
        return tuple(n_re), tuple(n_im)

    s_re, s_im = lax.fori_loop(0, time_tile, body, (s_re0, s_im0), unroll=4)
    for p in range(nvr):
        state_ref[:, lanes(p, False)] = s_re[p]
        state_ref[:, lanes(p, True)] = s_im[p]

    y = jnp.dot(bu_ref[...].astype(BF16), cm_ref[0], preferred_element_type=F32)
    y = y + d_ref[...] * x
    h_ref[...] = _gelu_tanh(y).astype(h_ref.dtype)


def _s5_operators(a_re, a_im, log_dt, b_re, b_im, c_re, c_im):
    g, p = a_re.shape
    h = b_re.shape[-1]
    gl = V7X_LANES // h
    nj = g // gl
    dt = jnp.exp(log_dt.astype(F32))[:, None]
    lam_re = jnp.minimum(a_re.astype(F32), LAM_RE_MAX)
    lam_im = a_im.astype(F32)
    mag = jnp.exp(lam_re * dt)
    lbar_re = mag * jnp.cos(lam_im * dt)
    lbar_im = mag * jnp.sin(lam_im * dt)
    den = lam_re * lam_re + lam_im * lam_im
    num_re = lbar_re - 1.0
    num_im = lbar_im
    coef_re = (num_re * lam_re + num_im * lam_im) / den
    coef_im = (num_im * lam_re - num_re * lam_im) / den
    cr, ci = coef_re[..., None], coef_im[..., None]
    br, bi = b_re.astype(F32), b_im.astype(F32)
    bb_re = cr * br - ci * bi
    bb_im = cr * bi + ci * br
    eye = jnp.eye(gl, dtype=F32)

    def b_blocks(bb):
        t = bb.transpose(0, 2, 1).reshape(nj, gl, h, p)
        return jnp.einsum('jahp,ab->jahbp', t, eye).reshape(nj, gl * h, gl * p)

    def c_blocks(c):
        t = c.astype(F32).reshape(nj, gl, h, p)
        return jnp.einsum('jahp,ab->japbh', t, eye).reshape(nj, gl * p, gl * h)

    bm = jnp.concatenate([b_blocks(bb_re), b_blocks(bb_im)], axis=2).astype(BF16)
    cm = jnp.concatenate([c_blocks(c_re), -c_blocks(c_im)], axis=1).astype(BF16)
    are = jnp.broadcast_to(lbar_re.reshape(nj, 1, gl * p), (nj, V7X_SUBLANES, gl * p))
    aim = jnp.broadcast_to(lbar_im.reshape(nj, 1, gl * p), (nj, V7X_SUBLANES, gl * p))
    return bm, cm, are, aim


def _s5_scan(x_tm, bm, cm, are, aim, d_skip, *, batch, time_tile):
    rows, d = x_tm.shape
    seq = rows // batch
    nj = d // V7X_LANES
    nstate = bm.shape[2]
    blk = time_tile * batch
    kern = functools.partial(_s5_kernel, time_tile=time_tile, batch=batch)
    return pl.pallas_call(
        kern,
        out_shape=jax.ShapeDtypeStruct((rows, d), BF16),
        grid=(nj, seq // time_tile),
        in_specs=[
            pl.BlockSpec((blk, V7X_LANES), lambda j, c: (c, j)),
            pl.BlockSpec((1, V7X_LANES, nstate), lambda j, c: (j, 0, 0)),
            pl.BlockSpec((1, nstate, V7X_LANES), lambda j, c: (j, 0, 0)),
            pl.BlockSpec((1, V7X_SUBLANES, nstate // 2), lambda j, c: (j, 0, 0)),
            pl.BlockSpec((1, V7X_SUBLANES, nstate // 2), lambda j, c: (j, 0, 0)),
            pl.BlockSpec((1, V7X_LANES), lambda j, c: (0, j)),
        ],
        out_specs=pl.BlockSpec((blk, V7X_LANES), lambda j, c: (c, j)),
        scratch_shapes=[pltpu.VMEM((batch, nstate), F32),
                        pltpu.VMEM((blk, nstate), F32)],
        compiler_params=_params("arbitrary", "arbitrary"),
        name="s5_scan",
    )(x_tm, bm, cm, are, aim, d_skip)


def _glu_out_ln_kernel(h_ref, x_ref, wglu_ref, wout_ref, g_ref, b_ref, o_ref, *, alpha):
    d = o_ref.shape[-1]
    vg = jnp.dot(h_ref[...], wglu_ref[...], preferred_element_type=F32)
    z = vg[:, :d] * _sigmoid(vg[:, d:])
    mix = jnp.dot(z.astype(BF16), wout_ref[...], preferred_element_type=F32)
    r = alpha * x_ref[...] + mix
    o_ref[...] = _layer_norm(r, g_ref[...], b_ref[...])


def _glu_out_ln(h_tm, x, w_glu, w_out, g, b, *, alpha, row_tile):
    bsz, seq, d = x.shape
    kern = functools.partial(_glu_out_ln_kernel, alpha=alpha)
    return pl.pallas_call(
        kern,
        out_shape=jax.ShapeDtypeStruct((bsz, seq, d), F32),
        grid=(bsz, seq // row_tile),
        in_specs=[
            pl.BlockSpec((row_tile, d), lambda bi, i: (i, bi)),
            pl.BlockSpec((None, row_tile, d), lambda bi, i: (bi, i, 0)),
            _resident(w_glu.shape),
            _resident(w_out.shape),
            _resident(g.shape),
            _resident(b.shape),
        ],
        out_specs=pl.BlockSpec((None, row_tile, d), lambda bi, i: (bi, i, 0)),
        compiler_params=_params("arbitrary", "arbitrary"),
        name="glu_out_ln",
    )(h_tm, x, w_glu, w_out, g, b)


def _ffn_ln_kernel(x_ref, wgu_ref, wdown_ref, g_ref, b_ref, o_ref, *, alpha):
    f = wdown_ref.shape[0]
    x = x_ref[...]
    gu = jnp.dot(x.astype(BF16), wgu_ref[...], preferred_element_type=F32)
    gate = gu[:, :f]
    act = gate * _sigmoid(gate) * gu[:, f:]
    y = jnp.dot(act.astype(BF16), wdown_ref[...], preferred_element_type=F32)
    o_ref[...] = _layer_norm(alpha * x + y, g_ref[...], b_ref[...])


def _ffn_ln(x2d, w_gu, w_down, g, b, *, alpha, row_tile):
    rows, d = x2d.shape
    kern = functools.partial(_ffn_ln_kernel, alpha=alpha)
    return pl.pallas_call(
        kern,
        out_shape=jax.ShapeDtypeStruct((rows, d), F32),
        grid=(rows // row_tile,),
        in_specs=[
            pl.BlockSpec((row_tile, d), lambda i: (i, 0)),
            _resident(w_gu.shape),
            _resident(w_down.shape),
            _resident(g.shape),
            _resident(b.shape),
        ],
        out_specs=pl.BlockSpec((row_tile, d), lambda i: (i, 0)),
        compiler_params=_params("arbitrary"),
        name="ffn_ln",
    )(x2d, w_gu, w_down, g, b)


def _qkv_kernel(x_ref, w_ref, o_ref, *, d, scale):
    acc = jnp.dot(x_ref[...].astype(BF16), w_ref[...], preferred_element_type=F32)
    o_ref[:, :d] = (acc[:, :d] * scale).astype(o_ref.dtype)
    o_ref[:, d:] = acc[:, d:].astype(o_ref.dtype)


def _qkv_proj(x2d, w_qkv, *, row_tile):
    rows, d = x2d.shape
    kern = functools.partial(_qkv_kernel, d=d, scale=HEAD_DIM ** -0.5)
    return pl.pallas_call(
        kern,
        out_shape=jax.ShapeDtypeStruct((rows, 3 * d), BF16),
        grid=(rows // row_tile,),
        in_specs=[pl.BlockSpec((row_tile, d), lambda i: (i, 0)),
                  _resident(w_qkv.shape)],
        out_specs=pl.BlockSpec((row_tile, 3 * d), lambda i: (i, 0)),
        compiler_params=_params("arbitrary"),
        name="qkv_proj",
    )(x2d, w_qkv)


def _attn_kernel(q_ref, k_ref, v_ref, u_ref, o_ref, acc_ref, tot_ref, *, q_tile, k_tile):
    i = pl.program_id(2)
    q2 = q_ref[...]
    lane = lax.broadcasted_iota(jnp.int32, q2.shape, 1)
    zero = jnp.zeros_like(q2)
    q_heads = (jnp.where(lane < HEAD_DIM, q2, zero), jnp.where(lane >= HEAD_DIM, q2, zero))
    umat = u_ref[...]
    acc_ref[...] = jnp.zeros_like(acc_ref)
    tot_ref[...] = jnp.zeros_like(tot_ref)
    diag_tiles = q_tile // k_tile
    n_kv = (i + 1) * diag_tiles
    q_pos = i * q_tile + lax.broadcasted_iota(jnp.int32, (q_tile, k_tile), 0)
    k_off = lax.broadcasted_iota(jnp.int32, (q_tile, k_tile), 1)

    def sweep(j, masked):
        k0 = pl.multiple_of(j * k_tile, k_tile)
        kj = k_ref[pl.ds(k0, k_tile), :]
        vj = v_ref[pl.ds(k0, k_tile), :]
        if masked:
            past = (k0 + k_off) < q_pos
        for hd in range(2):
            z = lax.dot_general(q_heads[hd], kj, (((1,), (1,)), ((), ())),
                                preferred_element_type=F32)
            sp = jnp.maximum(z, 0.0) + jnp.log(1.0 + jnp.exp(-jnp.abs(z)))
            if masked:
                sp = jnp.where(past, sp, 0.0)
            hi = sp.astype(BF16)
            lo = (sp - hi.astype(F32)).astype(BF16)
            suffix = (jnp.dot(hi, umat, preferred_element_type=F32)
                      + jnp.dot(lo, umat, preferred_element_type=F32))
            w = jnp.exp(z - suffix - tot_ref[hd])
            if masked:
                w = jnp.where(past, w, 0.0)
            acc_ref[hd] += jnp.dot(w.astype(BF16), vj, preferred_element_type=F32)
            tot_ref[hd] += suffix[:, 0:1]

    for dj in range(diag_tiles):
        sweep(n_kv - 1 - dj, True)

    def body(jj, carry):
        sweep(n_kv - diag_tiles - 1 - jj, False)
        return carry

    lax.fori_loop(0, n_kv - diag_tiles, body, 0)
    out = jnp.where(lane < HEAD_DIM, acc_ref[0], acc_ref[1])
    o_ref[...] = out.astype(o_ref.dtype)


def _stick_breaking(qkv, *, q_tile, k_tile):
    bsz, seq, d3 = qkv.shape
    d = d3 // 3
    pairs = d // V7X_LANES
    rr = lax.broadcasted_iota(jnp.int32, (k_tile, k_tile), 0)
    cc = lax.broadcasted_iota(jnp.int32, (k_tile, k_tile), 1)
    umat = (rr >= cc).astype(BF16)
    kern = functools.partial(_attn_kernel, q_tile=q_tile, k_tile=k_tile)
    return pl.pallas_call(
        kern,
        out_shape=jax.ShapeDtypeStruct((bsz, seq, d), BF16),
        grid=(bsz, pairs, seq // q_tile),
        in_specs=[
            pl.BlockSpec((None, q_tile, V7X_LANES), lambda b, p, i: (b, i, p)),
            pl.BlockSpec((None, seq, V7X_LANES), lambda b, p, i: (b, 0, pairs + p)),
            pl.BlockSpec((None, seq, V7X_LANES), lambda b, p, i: (b, 0, 2 * pairs + p)),
            _resident(umat.shape),
        ],
        out_specs=pl.BlockSpec((None, q_tile, V7X_LANES), lambda b, p, i: (b, i, p)),
        scratch_shapes=[pltpu.VMEM((2, q_tile, V7X_LANES), F32),
                        pltpu.VMEM((2, q_tile, 1), F32)],
        compiler_params=_params("arbitrary", "arbitrary", "arbitrary"),
        name="stick_breaking",
    )(qkv, qkv, qkv, umat)


def _proj_ln_kernel(o_ref, x_ref, w_ref, g_ref, b_ref, out_ref, *, alpha):
    mix = jnp.dot(o_ref[...], w_ref[...], preferred_element_type=F32)
    out_ref[...] = _layer_norm(alpha * x_ref[...] + mix, g_ref[...], b_ref[...])


def _proj_ln(o2d, x2d, w_o, g, b, *, alpha, row_tile):
    rows, d = x2d.shape
    kern = functools.partial(_proj_ln_kernel, alpha=alpha)
    return pl.pallas_call(
        kern,
        out_shape=jax.ShapeDtypeStruct((rows, d), F32),
        grid=(rows // row_tile,),
        in_specs=[
            pl.BlockSpec((row_tile, d), lambda i: (i, 0)),
            pl.BlockSpec((row_tile, d), lambda i: (i, 0)),
            _resident(w_o.shape),
            _resident(g.shape),
            _resident(b.shape),
        ],
        out_specs=pl.BlockSpec((row_tile, d), lambda i: (i, 0)),
        compiler_params=_params("arbitrary"),
        name="proj_ln",
    )(o2d, x2d, w_o, g, b)


def _tile(n, pref):
    return pref if n % pref == 0 else n


def kernel(x, ssm_a_re, ssm_a_im, ssm_log_dt, ssm_b_re, ssm_b_im, ssm_c_re, ssm_c_im, ssm_d,
           ssm_w_glu, ssm_w_out, sb_w_qkv, sb_w_o, ffn_w_gu, ffn_w_down,
           ln_mix_g, ln_mix_b, ln_ffn_g, ln_ffn_b):
    bsz, seq, d = x.shape
    depth = ffn_w_gu.shape[0]
    assert bsz == V7X_SUBLANES and d % V7X_LANES == 0
    alpha = (2 * depth) ** 0.25
    rows = bsz * seq
    row_tile = _tile(seq, ROW_TILE)
    ffn_tile = _tile(rows, FFN_ROW_TILE)
    vec = lambda a: a.reshape(1, d).astype(F32)

    h = x.astype(F32)
    for i in range(depth):
        j = i // 2
        if i % 2 == 0:
            bm, cm, are, aim = _s5_operators(ssm_a_re[j], ssm_a_im[j], ssm_log_dt[j],
                                             ssm_b_re[j], ssm_b_im[j], ssm_c_re[j], ssm_c_im[j])
            x_tm = h.transpose(1, 0, 2).reshape(rows, d)
            y_tm = _s5_scan(x_tm, bm, cm, are, aim, vec(ssm_d[j]), batch=bsz,
                            time_tile=_tile(seq, S5_TIME_TILE))
            h = _glu_out_ln(y_tm.reshape(seq, bsz * d), h, ssm_w_glu[j].astype(BF16),
                            ssm_w_out[j].astype(BF16), vec(ln_mix_g[i]), vec(ln_mix_b[i]),
                            alpha=alpha, row_tile=row_tile)
        else:
            h2d = h.reshape(rows, d)
            qkv = _qkv_proj(h2d, sb_w_qkv[j].astype(BF16), row_tile=_tile(rows, ROW_TILE))
            o = _stick_breaking(qkv.reshape(bsz, seq, 3 * d),
                                q_tile=_tile(seq, ATTN_Q_TILE),
                                k_tile=min(ATTN_K_TILE, _tile(seq, ATTN_Q_TILE)))
            h = _proj_ln(o.reshape(rows, d), h2d, sb_w_o[j].astype(BF16),
                         vec(ln_mix_g[i]), vec(ln_mix_b[i]), alpha=alpha,
                         row_tile=_tile(rows, ROW_TILE)).reshape(bsz, seq, d)
        h = _ffn_ln(h.reshape(rows, d), ffn_w_gu[i].astype(BF16), ffn_w_down[i].astype(BF16),
                    vec(ln_ffn_g[i]), vec(ln_ffn_b[i]), alpha=alpha,
                    row_tile=ffn_tile).reshape(bsz, seq, d)
    return h
```

```python
import functools
import math

import jax
import jax.numpy as jnp
import numpy as np
from jax import lax
from jax.experimental import pallas as pl
from jax.experimental.pallas import tpu as pltpu

F32 = jnp.float32
BF16 = jnp.bfloat16

V7X_LANES = 128
V7X_SUBLANES = 8
V7X_VMEM_LIMIT_BYTES = 56 * 1024 * 1024

SSM_GROUP = 16
HEAD_DIM = 64
LN_EPS = 1e-5
LAM_RE_MAX = -1e-4

S5_TIME_TILE = 512
S5_SUB_STEPS = 128
ROW_TILE = 512
FFN_ROW_TILE = 256
ATTN_Q_TILE = 512
ATTN_K_TILE = 256


def _params(*sem):
    return pltpu.CompilerParams(dimension_semantics=sem,
                                vmem_limit_bytes=V7X_VMEM_LIMIT_BYTES)


def _resident(shape):
    zeros = (0,) * len(shape)
    return pl.BlockSpec(shape, lambda *_: zeros, pipeline_mode=pl.Buffered(1))


def _sigmoid(x):
    return 1.0 / (1.0 + jnp.exp(-x))


def _gelu_tanh(x):
    c = math.sqrt(2.0 / math.pi)
    return 0.5 * x * (1.0 + jnp.tanh(c * (x + 0.044715 * (x * x * x))))


def _layer_norm(r, g, b):
    mu = jnp.mean(r, axis=-1, keepdims=True)
    d = r - mu
    var = jnp.mean(d * d, axis=-1, keepdims=True)
    return d * lax.rsqrt(var + LN_EPS) * g + b


def _s5_kernel(x_ref, bm_ref, cm_ref, are_ref, aim_ref, d_ref, h_ref,
               state_ref, buf_a, buf_b, xs_ref, ys_ref, *, sub_steps, n_sub, batch):
    half = buf_a.shape[1] // 2
    nvr = half // V7X_LANES
    sub_rows = sub_steps * batch
    bufs = (buf_a, buf_b)

    @pl.when(pl.program_id(1) == 0)
    def _():
        state_ref[...] = jnp.zeros_like(state_ref)

    def lanes(p, imag):
        lo = (half if imag else 0) + p * V7X_LANES
        return slice(lo, lo + V7X_LANES)

    def sub(k):
        return slice(k * sub_rows, (k + 1) * sub_rows)

    def steps(k):
        return slice(k * sub_steps, (k + 1) * sub_steps)

    def project_in(k, buf):
        for b in range(batch):
            xs_ref[pl.ds(k * sub_rows + b, sub_steps, stride=batch), :] = x_ref[b, steps(k), :]
        buf[...] = jnp.dot(xs_ref[sub(k), :].astype(BF16), bm_ref[0],
                           preferred_element_type=F32)

    def project_out(k, buf):
        y = jnp.dot(buf[...].astype(BF16), cm_ref[0], preferred_element_type=F32)
        y = y + d_ref[...] * xs_ref[sub(k), :]
        ys_ref[...] = _gelu_tanh(y)
        for b in range(batch):
            h_ref[b, steps(k), :] = ys_ref[pl.ds(b, sub_steps, stride=batch), :].astype(
                h_ref.dtype)

    a_re = [are_ref[0, :, lanes(p, False)] for p in range(nvr)]
    a_im = [aim_ref[0, :, lanes(p, False)] for p in range(nvr)]
    s_re = [state_ref[:, lanes(p, False)] for p in range(nvr)]
    s_im = [state_ref[:, lanes(p, True)] for p in range(nvr)]

    def recur(buf):
        for t in range(sub_steps):
            rows = slice(t * batch, (t + 1) * batch)
            for p in range(nvr):
                v_re = a_re[p] * s_re[p] - a_im[p] * s_im[p] + buf[rows, lanes(p, False)]
                v_im = a_re[p] * s_im[p] + a_im[p] * s_re[p] + buf[rows, lanes(p, True)]
                buf[rows, lanes(p, False)] = v_re
                buf[rows, lanes(p, True)] = v_im
                s_re[p], s_im[p] = v_re, v_im

    project_in(0, bufs[0])
    for k in range(n_sub):
        cur, other = bufs[k % 2], bufs[(k + 1) % 2]
        if k >= 1:
            project_out(k - 1, other)
        if k + 1 < n_sub:
            project_in(k + 1, other)
        recur(cur)
    project_out(n_sub - 1, bufs[(n_sub - 1) % 2])

    for p in range(nvr):
        state_ref[:, lanes(p, False)] = s_re[p]
        state_ref[:, lanes(p, True)] = s_im[p]


def _s5_operators(a_re, a_im, log_dt, b_re, b_im, c_re, c_im):
    g, p = a_re.shape
    h = b_re.shape[-1]
    gl = V7X_LANES // h
    nj = g // gl
    dt = jnp.exp(log_dt.astype(F32))[:, None]
    lam_re = jnp.minimum(a_re.astype(F32), LAM_RE_MAX)
    lam_im = a_im.astype(F32)
    mag = jnp.exp(lam_re * dt)
    lbar_re = mag * jnp.cos(lam_im * dt)
    lbar_im = mag * jnp.sin(lam_im * dt)
    den = lam_re * lam_re + lam_im * lam_im
    num_re = lbar_re - 1.0
    num_im = lbar_im
    coef_re = (num_re * lam_re + num_im * lam_im) / den
    coef_im = (num_im * lam_re - num_re * lam_im) / den
    cr, ci = coef_re[..., None], coef_im[..., None]
    br, bi = b_re.astype(F32), b_im.astype(F32)
    bb_re = cr * br - ci * bi
    bb_im = cr * bi + ci * br
    eye = jnp.eye(gl, dtype=F32)

    def b_blocks(bb):
        t = bb.transpose(0, 2, 1).reshape(nj, gl, h, p)
        return jnp.einsum('jahp,ab->jahbp', t, eye).reshape(nj, gl * h, gl * p)

    def c_blocks(c):
        t = c.astype(F32).reshape(nj, gl, h, p)
        return jnp.einsum('jahp,ab->japbh', t, eye).reshape(nj, gl * p, gl * h)

    bm = jnp.concatenate([b_blocks(bb_re), b_blocks(bb_im)], axis=2).astype(BF16)
    cm = jnp.concatenate([c_blocks(c_re), -c_blocks(c_im)], axis=1).astype(BF16)
    are = jnp.broadcast_to(lbar_re.reshape(nj, 1, gl * p), (nj, V7X_SUBLANES, gl * p))
    aim = jnp.broadcast_to(lbar_im.reshape(nj, 1, gl * p), (nj, V7X_SUBLANES, gl * p))
    return bm, cm, are, aim


def _s5_scan(x, bm, cm, are, aim, d_skip, *, time_tile, sub_steps):
    batch, seq, d = x.shape
    nj = d // V7X_LANES
    nstate = bm.shape[2]
    sub_rows = sub_steps * batch
    kern = functools.partial(_s5_kernel, sub_steps=sub_steps, n_sub=time_tile // sub_steps,
                             batch=batch)
    io_block = pl.BlockSpec((batch, time_tile, V7X_LANES), lambda j, c: (0, c, j))
    return pl.pallas_call(
        kern,
        out_shape=jax.ShapeDtypeStruct((batch, seq, d), BF16),
        grid=(nj, seq // time_tile),
        in_specs=[
            io_block,
            pl.BlockSpec((1, V7X_LANES, nstate), lambda j, c: (j, 0, 0)),
            pl.BlockSpec((1, nstate, V7X_LANES), lambda j, c: (j, 0, 0)),
            pl.BlockSpec((1, V7X_SUBLANES, nstate // 2), lambda j, c: (j, 0, 0)),
            pl.BlockSpec((1, V7X_SUBLANES, nstate // 2), lambda j, c: (j, 0, 0)),
            pl.BlockSpec((1, V7X_LANES), lambda j, c: (0, j)),
        ],
        out_specs=io_block,
        scratch_shapes=[pltpu.VMEM((batch, nstate), F32),
                        pltpu.VMEM((sub_rows, nstate), F32),
                        pltpu.VMEM((sub_rows, nstate), F32),
                        pltpu.VMEM((time_tile * batch, V7X_LANES), F32),
                        pltpu.VMEM((sub_rows, V7X_LANES), F32)],
        compiler_params=_params("arbitrary", "arbitrary"),
        name="s5_scan",
    )(x, bm, cm, are, aim, d_skip)


def _glu_out_ln_kernel(h_ref, x_ref, wglu_ref, wout_ref, g_ref, b_ref, o_ref, *, alpha):
    d = o_ref.shape[-1]
    vg = jnp.dot(h_ref[...], wglu_ref[...], preferred_element_type=F32)
    z = vg[:, :d] * _sigmoid(vg[:, d:])
    mix = jnp.dot(z.astype(BF16), wout_ref[...], preferred_element_type=F32)
    r = alpha * x_ref[...] + mix
    o_ref[...] = _layer_norm(r, g_ref[...], b_ref[...])


def _glu_out_ln(h2d, x2d, w_glu, w_out, g, b, *, alpha, row_tile):
    rows, d = x2d.shape
    kern = functools.partial(_glu_out_ln_kernel, alpha=alpha)
    return pl.pallas_call(
        kern,
        out_shape=jax.ShapeDtypeStruct((rows, d), F32),
        grid=(rows // row_tile,),
        in_specs=[
            pl.BlockSpec((row_tile, d), lambda i: (i, 0)),
            pl.BlockSpec((row_tile, d), lambda i: (i, 0)),
            _resident(w_glu.shape),
            _resident(w_out.shape),
            _resident(g.shape),
            _resident(b.shape),
        ],
        out_specs=pl.BlockSpec((row_tile, d), lambda i: (i, 0)),
        compiler_params=_params("arbitrary"),
        name="glu_out_ln",
    )(h2d, x2d, w_glu, w_out, g, b)


def _ffn_ln_kernel(x_ref, wgu_ref, wdown_ref, g_ref, b_ref, o_ref, *, alpha):
    f = wdown_ref.shape[0]
    x = x_ref[...]
    gu = jnp.dot(x.astype(BF16), wgu_ref[...], preferred_element_type=F32)
    gate = gu[:, :f]
    act = gate * _sigmoid(gate) * gu[:, f:]
    y = jnp.dot(act.astype(BF16), wdown_ref[...], preferred_element_type=F32)
    o_ref[...] = _layer_norm(alpha * x + y, g_ref[...], b_ref[...])


def _ffn_ln(x2d, w_gu, w_down, g, b, *, alpha, row_tile):
    rows, d = x2d.shape
    kern = functools.partial(_ffn_ln_kernel, alpha=alpha)
    return pl.pallas_call(
        kern,
        out_shape=jax.ShapeDtypeStruct((rows, d), F32),
        grid=(rows // row_tile,),
        in_specs=[
            pl.BlockSpec((row_tile, d), lambda i: (i, 0)),
            _resident(w_gu.shape),
            _resident(w_down.shape),
            _resident(g.shape),
            _resident(b.shape),
        ],
        out_specs=pl.BlockSpec((row_tile, d), lambda i: (i, 0)),
        compiler_params=_params("arbitrary"),
        name="ffn_ln",
    )(x2d, w_gu, w_down, g, b)


def _qkv_kernel(x_ref, w_ref, o_ref, *, d, scale):
    acc = jnp.dot(x_ref[...].astype(BF16), w_ref[...], preferred_element_type=F32)
    o_ref[:, :d] = (acc[:, :d] * scale).astype(o_ref.dtype)
    o_ref[:, d:] = acc[:, d:].astype(o_ref.dtype)


def _qkv_proj(x2d, w_qkv, *, row_tile):
    rows, d = x2d.shape
    kern = functools.partial(_qkv_kernel, d=d, scale=HEAD_DIM ** -0.5)
    return pl.pallas_call(
        kern,
        out_shape=jax.ShapeDtypeStruct((rows, 3 * d), BF16),
        grid=(rows // row_tile,),
        in_specs=[pl.BlockSpec((row_tile, d), lambda i: (i, 0)),
                  _resident(w_qkv.shape)],
        out_specs=pl.BlockSpec((row_tile, 3 * d), lambda i: (i, 0)),
        compiler_params=_params("arbitrary"),
        name="qkv_proj",
    )(x2d, w_qkv)


ATTN_STAGES = 5
ATTN_UNROLL = 8
ATTN_SLOTS = 4
ATTN_UNCAPPED = 3e38
TAB_QROW, TAB_KROW, TAB_MASK, TAB_FIRST, TAB_LAST = range(5)


def _attn_schedule(seq, q_tile, k_tile):
    diag = q_tile // k_tile
    tiles = []
    for i in range(seq // q_tile):
        n_kv = (i + 1) * diag
        for jj in range(n_kv):
            tiles.append((i * q_tile, (n_kv - 1 - jj) * k_tile, min(jj, diag),
                          int(jj == 0), int(jj == n_kv - 1)))
    assert len(tiles) % 2 == 0, "key tiles per sweep must be even"
    idle = (0, 0, diag + 1, 0, 0)
    tiles_per_body = ATTN_UNROLL // 2
    n_body = -(-(2 * len(tiles) + ATTN_STAGES - 1) // ATTN_UNROLL)
    tail = tiles_per_body * n_body - len(tiles)
    table = np.asarray([idle] * 2 + tiles + [idle] * tail, np.int32).T
    return table, n_body


def _attn_masks(q_tile, k_tile):
    diag = q_tile // k_tile
    r = lax.broadcasted_iota(jnp.int32, (q_tile, k_tile), 0)
    c = lax.broadcasted_iota(jnp.int32, (q_tile, k_tile), 1)
    allowed = [((diag - 1 - t) * k_tile + c) < r for t in range(diag)]
    allowed += [jnp.ones((q_tile, k_tile), bool), jnp.zeros((q_tile, k_tile), bool)]
    cap = jnp.where(jnp.stack(allowed), ATTN_UNCAPPED, 0.0).astype(F32)
    return cap, cap.astype(BF16)


def _attn_kernel(tab_ref, q_ref, k_ref, v_ref, u_ref, cap_ref, wcap_ref, o_ref,
                 qh_ref, z_buf, hi_buf, lo_buf, s_buf, w_buf, acc_ref, tot_ref,
                 *, q_tile, k_tile, n_body):
    q2 = q_ref[...]
    lane = lax.broadcasted_iota(jnp.int32, q2.shape, 1)
    zero = jnp.zeros_like(q2)
    qh_ref[0] = jnp.where(lane < HEAD_DIM, q2, zero)
    qh_ref[1] = jnp.where(lane >= HEAD_DIM, q2, zero)
    for buf in (z_buf, hi_buf, lo_buf, s_buf, w_buf, acc_ref, tot_ref):
        buf[...] = jnp.zeros_like(buf)
    umat = u_ref[...]
    nt = (((1,), (1,)), ((), ()))

    def keep(e):
        return (1 - tab_ref[TAB_FIRST, e]).astype(F32)

    def k_rows(e):
        return pl.ds(pl.multiple_of(tab_ref[TAB_KROW, e], k_tile), k_tile)

    def q_rows(e):
        return pl.ds(pl.multiple_of(tab_ref[TAB_QROW, e], q_tile), q_tile)

    def stage_qk(e, hd, zslot, slot):
        z_buf[zslot] = lax.dot_general(qh_ref[hd, q_rows(e), :], k_ref[k_rows(e), :], nt,
                                       preferred_element_type=F32)

    def stage_split(e, hd, zslot, slot):
        z = z_buf[zslot]
        sp = jnp.maximum(z, 0.0) + jnp.log(1.0 + jnp.exp(-jnp.abs(z)))
        sp = jnp.minimum(sp, cap_ref[tab_ref[TAB_MASK, e]])
        hi = sp.astype(BF16)
        hi_buf[slot] = hi
        lo_buf[slot] = (sp - hi.astype(F32)).astype(BF16)

    def stage_suffix(e, hd, zslot, slot):
        s_buf[slot] = (jnp.dot(hi_buf[slot], umat, preferred_element_type=F32)
                       + jnp.dot(lo_buf[slot], umat, preferred_element_type=F32))

    def stage_weights(e, hd, zslot, slot):
        tot = tot_ref[hd] * keep(e)
        suffix = s_buf[slot]
        w = jnp.exp(z_buf[zslot] - suffix - tot).astype(BF16)
        w_buf[slot] = jnp.minimum(w, wcap_ref[tab_ref[TAB_MASK, e]])
        tot_ref[hd] = tot + suffix[:, 0:1]

    def stage_pv(e, hd, zslot, slot):
        acc_ref[hd] += jnp.dot(w_buf[slot], v_ref[k_rows(e), :], preferred_element_type=F32)

    stages = (stage_qk, stage_split, stage_suffix, stage_weights, stage_pv)
    assert len(stages) == ATTN_STAGES

    def flush_if_last(done):
        @pl.when(tab_ref[TAB_LAST, done] == 1)
        def _():
            tile_lane = lax.broadcasted_iota(jnp.int32, (q_tile, V7X_LANES), 1)
            out = jnp.where(tile_lane < HEAD_DIM, acc_ref[0], acc_ref[1])
            o_ref[q_rows(done), :] = out.astype(o_ref.dtype)
            acc_ref[...] = jnp.zeros_like(acc_ref)

    def body(m, carry):
        base = (ATTN_UNROLL // 2) * m + 2
        for u in range(ATTN_UNROLL):
            for k, stage in enumerate(stages):
                item = u - k
                stage(base + item // 2, item % 2, item % ATTN_UNROLL, item % ATTN_SLOTS)
            if u % 4 == 3:
                flush_if_last(base + (u - 4) // 2)
        return carry

    lax.fori_loop(0, n_body, body, 0)


def _stick_breaking(qkv, *, q_tile, k_tile):
    bsz, seq, d3 = qkv.shape
    d = d3 // 3
    pairs = d // V7X_LANES
    rr = lax.broadcasted_iota(jnp.int32, (k_tile, k_tile), 0)
    cc = lax.broadcasted_iota(jnp.int32, (k_tile, k_tile), 1)
    umat = (rr >= cc).astype(BF16)
    table, n_body = _attn_schedule(seq, q_tile, k_tile)
    cap, wcap = _attn_masks(q_tile, k_tile)
    kern = functools.partial(_attn_kernel, q_tile=q_tile, k_tile=k_tile, n_body=n_body)
    seq_block = lambda off: pl.BlockSpec((None, seq, V7X_LANES),
                                         lambda b, p, tab: (b, 0, off * pairs + p))
    grid_spec = pltpu.PrefetchScalarGridSpec(
        num_scalar_prefetch=1,
        grid=(bsz, pairs),
        in_specs=[seq_block(0), seq_block(1), seq_block(2),
                  _resident(umat.shape), _resident(cap.shape), _resident(wcap.shape)],
        out_specs=seq_block(0),
        scratch_shapes=[
            pltpu.VMEM((2, seq, V7X_LANES), BF16),
            pltpu.VMEM((ATTN_UNROLL, q_tile, k_tile), F32),
            pltpu.VMEM((ATTN_SLOTS, q_tile, k_tile), BF16),
            pltpu.VMEM((ATTN_SLOTS, q_tile, k_tile), BF16),
            pltpu.VMEM((ATTN_SLOTS, q_tile, k_tile), F32),
            pltpu.VMEM((ATTN_SLOTS, q_tile, k_tile), BF16),
            pltpu.VMEM((2, q_tile, V7X_LANES), F32),
            pltpu.VMEM((2, q_tile, 1), F32),
        ])
    return pl.pallas_call(
        kern,
        out_shape=jax.ShapeDtypeStruct((bsz, seq, d), BF16),
        grid_spec=grid_spec,
        compiler_params=_params("arbitrary", "arbitrary"),
        name="stick_breaking",
    )(jnp.asarray(table), qkv, qkv, qkv, umat, cap, wcap)


def _proj_ln_kernel(o_ref, x_ref, w_ref, g_ref, b_ref, out_ref, *, alpha):
    mix = jnp.dot(o_ref[...], w_ref[...], preferred_element_type=F32)
    out_ref[...] = _layer_norm(alpha * x_ref[...] + mix, g_ref[...], b_ref[...])


def _proj_ln(o2d, x2d, w_o, g, b, *, alpha, row_tile):
    rows, d = x2d.shape
    kern = functools.partial(_proj_ln_kernel, alpha=alpha)
    return pl.pallas_call(
        kern,
        out_shape=jax.ShapeDtypeStruct((rows, d), F32),
        grid=(rows // row_tile,),
        in_specs=[
            pl.BlockSpec((row_tile, d), lambda i: (i, 0)),
            pl.BlockSpec((row_tile, d), lambda i: (i, 0)),
            _resident(w_o.shape),
            _resident(g.shape),
            _resident(b.shape),
        ],
        out_specs=pl.BlockSpec((row_tile, d), lambda i: (i, 0)),
        compiler_params=_params("arbitrary"),
        name="proj_ln",
    )(o2d, x2d, w_o, g, b)


def _tile(n, pref):
    return pref if n % pref == 0 else n


def kernel(x, ssm_a_re, ssm_a_im, ssm_log_dt, ssm_b_re, ssm_b_im, ssm_c_re, ssm_c_im, ssm_d,
           ssm_w_glu, ssm_w_out, sb_w_qkv, sb_w_o, ffn_w_gu, ffn_w_down,
           ln_mix_g, ln_mix_b, ln_ffn_g, ln_ffn_b):
    bsz, seq, d = x.shape
    depth = ffn_w_gu.shape[0]
    assert bsz == V7X_SUBLANES and d % V7X_LANES == 0
    alpha = (2 * depth) ** 0.25
    rows = bsz * seq
    row_tile = _tile(seq, ROW_TILE)
    ffn_tile = _tile(rows, FFN_ROW_TILE)
    vec = lambda a: a.reshape(1, d).astype(F32)

    h = x.astype(F32)
    for i in range(depth):
        j = i // 2
        if i % 2 == 0:
            bm, cm, are, aim = _s5_operators(ssm_a_re[j], ssm_a_im[j], ssm_log_dt[j],
                                             ssm_b_re[j], ssm_b_im[j], ssm_c_re[j], ssm_c_im[j])
            time_tile = _tile(seq, S5_TIME_TILE)
            y = _s5_scan(h, bm, cm, are, aim, vec(ssm_d[j]), time_tile=time_tile,
                         sub_steps=_tile(time_tile, S5_SUB_STEPS))
            h = _glu_out_ln(y.reshape(rows, d), h.reshape(rows, d), ssm_w_glu[j].astype(BF16),
                            ssm_w_out[j].astype(BF16), vec(ln_mix_g[i]), vec(ln_mix_b[i]),
                            alpha=alpha, row_tile=row_tile).reshape(bsz, seq, d)
        else:
            h2d = h.reshape(rows, d)
            qkv = _qkv_proj(h2d, sb_w_qkv[j].astype(BF16), row_tile=_tile(rows, ROW_TILE))
            o = _stick_breaking(qkv.reshape(bsz, seq, 3 * d),
                                q_tile=_tile(seq, ATTN_Q_TILE),
                                k_tile=min(ATTN_K_TILE, _tile(seq, ATTN_Q_TILE)))
            h = _proj_ln(o.reshape(rows, d), h2d, sb_w_o[j].astype(BF16),
                         vec(ln_mix_g[i]), vec(ln_mix_b[i]), alpha=alpha,
                         row_tile=_tile(rows, ROW_TILE)).reshape(bsz, seq, d)
        h = _ffn_ln(h.reshape(rows, d), ffn_w_gu[i].astype(BF16), ffn_w_down[i].astype(BF16),
                    vec(ln_ffn_g[i]), vec(ln_ffn_b[i]), alpha=alpha,
                    row_tile=ffn_tile).reshape(bsz, seq, d)
    return h
```

```python
import functools
import math

import jax
import jax.numpy as jnp
import numpy as np
from jax import lax
from jax.experimental import pallas as pl
from jax.experimental.pallas import tpu as pltpu

F32 = jnp.float32
BF16 = jnp.bfloat16

V7X_LANES = 128
V7X_SUBLANES = 8
V7X_VMEM_LIMIT_BYTES = 56 * 1024 * 1024

SSM_GROUP = 16
HEAD_DIM = 64
LN_EPS = 1e-5
LAM_RE_MAX = -1e-4

S5_TIME_TILE = 512
S5_SUB_STEPS = 128
ROW_TILE = 512
FFN_ROW_TILE = 256
ATTN_Q_TILE = 512
ATTN_K_TILE = 256


def _params(*sem):
    return pltpu.CompilerParams(dimension_semantics=sem,
                                vmem_limit_bytes=V7X_VMEM_LIMIT_BYTES)


def _resident(shape):
    zeros = (0,) * len(shape)
    return pl.BlockSpec(shape, lambda *_: zeros, pipeline_mode=pl.Buffered(1))


def _sigmoid(x):
    return 1.0 / (1.0 + jnp.exp(-x))


def _gelu_tanh(x):
    c = math.sqrt(2.0 / math.pi)
    return 0.5 * x * (1.0 + jnp.tanh(c * (x + 0.044715 * (x * x * x))))


def _layer_norm(r, g, b):
    mu = jnp.mean(r, axis=-1, keepdims=True)
    d = r - mu
    var = jnp.mean(d * d, axis=-1, keepdims=True)
    return d * lax.rsqrt(var + LN_EPS) * g + b


def _s5_kernel(x_ref, bm_ref, cm_ref, are_ref, aim_ref, d_ref, h_ref,
               state_ref, buf_a, buf_b, xs_ref, ys_ref, *, sub_steps, n_sub, batch):
    half = buf_a.shape[1] // 2
    nvr = half // V7X_LANES
    sub_rows = sub_steps * batch
    bufs = (buf_a, buf_b)

    @pl.when(pl.program_id(1) == 0)
    def _():
        state_ref[...] = jnp.zeros_like(state_ref)

    def lanes(p, imag):
        lo = (half if imag else 0) + p * V7X_LANES
        return slice(lo, lo + V7X_LANES)

    def sub(k):
        return slice(k * sub_rows, (k + 1) * sub_rows)

    def steps(k):
        return slice(k * sub_steps, (k + 1) * sub_steps)

    def project_in(k, buf):
        for b in range(batch):
            xs_ref[pl.ds(k * sub_rows + b, sub_steps, stride=batch), :] = x_ref[b, steps(k), :]
        buf[...] = jnp.dot(xs_ref[sub(k), :].astype(BF16), bm_ref[0],
                           preferred_element_type=F32)

    def project_out(k, buf):
        y = jnp.dot(buf[...].astype(BF16), cm_ref[0], preferred_element_type=F32)
        y = y + d_ref[...] * xs_ref[sub(k), :]
        ys_ref[...] = _gelu_tanh(y)
        for b in range(batch):
            h_ref[b, steps(k), :] = ys_ref[pl.ds(b, sub_steps, stride=batch), :].astype(
                h_ref.dtype)

    a_re = [are_ref[0, :, lanes(p, False)] for p in range(nvr)]
    a_im = [aim_ref[0, :, lanes(p, False)] for p in range(nvr)]
    s_re = [state_ref[:, lanes(p, False)] for p in range(nvr)]
    s_im = [state_ref[:, lanes(p, True)] for p in range(nvr)]

    def recur(buf):
        for t in range(sub_steps):
            rows = slice(t * batch, (t + 1) * batch)
            for p in range(nvr):
                v_re = a_re[p] * s_re[p] - a_im[p] * s_im[p] + buf[rows, lanes(p, False)]
                v_im = a_re[p] * s_im[p] + a_im[p] * s_re[p] + buf[rows, lanes(p, True)]
                buf[rows, lanes(p, False)] = v_re
                buf[rows, lanes(p, True)] = v_im
                s_re[p], s_im[p] = v_re, v_im

    project_in(0, bufs[0])
    for k in range(n_sub):
        cur, other = bufs[k % 2], bufs[(k + 1) % 2]
        if k >= 1:
            project_out(k - 1, other)
        if k + 1 < n_sub:
            project_in(k + 1, other)
        recur(cur)
    project_out(n_sub - 1, bufs[(n_sub - 1) % 2])

    for p in range(nvr):
        state_ref[:, lanes(p, False)] = s_re[p]
        state_ref[:, lanes(p, True)] = s_im[p]


def _s5_operators(a_re, a_im, log_dt, b_re, b_im, c_re, c_im):
    g, p = a_re.shape
    h = b_re.shape[-1]
    gl = V7X_LANES // h
    nj = g // gl
    dt = jnp.exp(log_dt.astype(F32))[:, None]
    lam_re = jnp.minimum(a_re.astype(F32), LAM_RE_MAX)
    lam_im = a_im.astype(F32)
    mag = jnp.exp(lam_re * dt)
    lbar_re = mag * jnp.cos(lam_im * dt)
    lbar_im = mag * jnp.sin(lam_im * dt)
    den = lam_re * lam_re + lam_im * lam_im
    num_re = lbar_re - 1.0
    num_im = lbar_im
    coef_re = (num_re * lam_re + num_im * lam_im) / den
    coef_im = (num_im * lam_re - num_re * lam_im) / den
    cr, ci = coef_re[..., None], coef_im[..., None]
    br, bi = b_re.astype(F32), b_im.astype(F32)
    bb_re = cr * br - ci * bi
    bb_im = cr * bi + ci * br
    eye = jnp.eye(gl, dtype=F32)

    def b_blocks(bb):
        t = bb.transpose(0, 2, 1).reshape(nj, gl, h, p)
        return jnp.einsum('jahp,ab->jahbp', t, eye).reshape(nj, gl * h, gl * p)

    def c_blocks(c):
        t = c.astype(F32).reshape(nj, gl, h, p)
        return jnp.einsum('jahp,ab->japbh', t, eye).reshape(nj, gl * p, gl * h)

    bm = jnp.concatenate([b_blocks(bb_re), b_blocks(bb_im)], axis=2).astype(BF16)
    cm = jnp.concatenate([c_blocks(c_re), -c_blocks(c_im)], axis=1).astype(BF16)
    are = jnp.broadcast_to(lbar_re.reshape(nj, 1, gl * p), (nj, V7X_SUBLANES, gl * p))
    aim = jnp.broadcast_to(lbar_im.reshape(nj, 1, gl * p), (nj, V7X_SUBLANES, gl * p))
    return bm, cm, are, aim


def _s5_scan(x, bm, cm, are, aim, d_skip, *, time_tile, sub_steps):
    batch, seq, d = x.shape
    nj = d // V7X_LANES
    nstate = bm.shape[2]
    sub_rows = sub_steps * batch
    kern = functools.partial(_s5_kernel, sub_steps=sub_steps, n_sub=time_tile // sub_steps,
                             batch=batch)
    io_block = pl.BlockSpec((batch, time_tile, V7X_LANES), lambda j, c: (0, c, j))
    return pl.pallas_call(
        kern,
        out_shape=jax.ShapeDtypeStruct((batch, seq, d), BF16),
        grid=(nj, seq // time_tile),
        in_specs=[
            io_block,
            pl.BlockSpec((1, V7X_LANES, nstate), lambda j, c: (j, 0, 0)),
            pl.BlockSpec((1, nstate, V7X_LANES), lambda j, c: (j, 0, 0)),
            pl.BlockSpec((1, V7X_SUBLANES, nstate // 2), lambda j, c: (j, 0, 0)),
            pl.BlockSpec((1, V7X_SUBLANES, nstate // 2), lambda j, c: (j, 0, 0)),
            pl.BlockSpec((1, V7X_LANES), lambda j, c: (0, j)),
        ],
        out_specs=io_block,
        scratch_shapes=[pltpu.VMEM((batch, nstate), F32),
                        pltpu.VMEM((sub_rows, nstate), F32),
                        pltpu.VMEM((sub_rows, nstate), F32),
                        pltpu.VMEM((time_tile * batch, V7X_LANES), F32),
                        pltpu.VMEM((sub_rows, V7X_LANES), F32)],
        compiler_params=_params("arbitrary", "arbitrary"),
        name="s5_scan",
    )(x, bm, cm, are, aim, d_skip)


def _glu_out_ln_kernel(h_ref, x_ref, wglu_ref, wout_ref, g_ref, b_ref, o_ref, *, alpha):
    d = o_ref.shape[-1]
    vg = jnp.dot(h_ref[...], wglu_ref[...], preferred_element_type=F32)
    z = vg[:, :d] * _sigmoid(vg[:, d:])
    mix = jnp.dot(z.astype(BF16), wout_ref[...], preferred_element_type=F32)
    r = alpha * x_ref[...] + mix
    o_ref[...] = _layer_norm(r, g_ref[...], b_ref[...])


def _glu_out_ln(h2d, x2d, w_glu, w_out, g, b, *, alpha, row_tile):
    rows, d = x2d.shape
    kern = functools.partial(_glu_out_ln_kernel, alpha=alpha)
    return pl.pallas_call(
        kern,
        out_shape=jax.ShapeDtypeStruct((rows, d), F32),
        grid=(rows // row_tile,),
        in_specs=[
            pl.BlockSpec((row_tile, d), lambda i: (i, 0)),
            pl.BlockSpec((row_tile, d), lambda i: (i, 0)),
            _resident(w_glu.shape),
            _resident(w_out.shape),
            _resident(g.shape),
            _resident(b.shape),
        ],
        out_specs=pl.BlockSpec((row_tile, d), lambda i: (i, 0)),
        compiler_params=_params("arbitrary"),
        name="glu_out_ln",
    )(h2d, x2d, w_glu, w_out, g, b)


def _ffn_ln_kernel(x_ref, wgu_ref, wdown_ref, g_ref, b_ref, o_ref, *, alpha):
    f = wdown_ref.shape[0]
    x = x_ref[...]
    gu = jnp.dot(x.astype(BF16), wgu_ref[...], preferred_element_type=F32)
    gate = gu[:, :f]
    act = gate * _sigmoid(gate) * gu[:, f:]
    y = jnp.dot(act.astype(BF16), wdown_ref[...], preferred_element_type=F32)
    o_ref[...] = _layer_norm(alpha * x + y, g_ref[...], b_ref[...])


def _ffn_ln(x2d, w_gu, w_down, g, b, *, alpha, row_tile):
    rows, d = x2d.shape
    kern = functools.partial(_ffn_ln_kernel, alpha=alpha)
    return pl.pallas_call(
        kern,
        out_shape=jax.ShapeDtypeStruct((rows, d), F32),
        grid=(rows // row_tile,),
        in_specs=[
            pl.BlockSpec((row_tile, d), lambda i: (i, 0)),
            _resident(w_gu.shape),
            _resident(w_down.shape),
            _resident(g.shape),
            _resident(b.shape),
        ],
        out_specs=pl.BlockSpec((row_tile, d), lambda i: (i, 0)),
        compiler_params=_params("arbitrary"),
        name="ffn_ln",
    )(x2d, w_gu, w_down, g, b)


def _qkv_kernel(x_ref, w_ref, o_ref, *, d, scale):
    acc = jnp.dot(x_ref[...].astype(BF16), w_ref[...], preferred_element_type=F32)
    o_ref[:, :d] = (acc[:, :d] * scale).astype(o_ref.dtype)
    o_ref[:, d:] = acc[:, d:].astype(o_ref.dtype)


def _qkv_proj(x2d, w_qkv, *, row_tile):
    rows, d = x2d.shape
    kern = functools.partial(_qkv_kernel, d=d, scale=HEAD_DIM ** -0.5)
    return pl.pallas_call(
        kern,
        out_shape=jax.ShapeDtypeStruct((rows, 3 * d), BF16),
        grid=(rows // row_tile,),
        in_specs=[pl.BlockSpec((row_tile, d), lambda i: (i, 0)),
                  _resident(w_qkv.shape)],
        out_specs=pl.BlockSpec((row_tile, 3 * d), lambda i: (i, 0)),
        compiler_params=_params("arbitrary"),
        name="qkv_proj",
    )(x2d, w_qkv)


ATTN_STAGES = 5
ATTN_UNROLL = 8
ATTN_SLOTS = 4
ATTN_MASKED_OUT = -1e30
NEG_LOG2E = -1.4426950408889634
TAB_QROW, TAB_KROW, TAB_MASK = range(3)


def _attn_schedule(seq, q_tile, k_tile):
    diag = q_tile // k_tile
    tiles = []
    for i in range(seq // q_tile):
        n_kv = (i + 1) * diag
        for jj in range(n_kv):
            tiles.append((i * q_tile, (n_kv - 1 - jj) * k_tile, min(jj, diag)))
    idle = (0, 0, diag + 1)
    lead = (ATTN_STAGES - 1) // 2
    tiles_per_body = ATTN_UNROLL // 2
    n_body = -(-(2 * len(tiles) + ATTN_STAGES - 1) // ATTN_UNROLL)
    tail = tiles_per_body * n_body - len(tiles)
    table = np.asarray([idle] * lead + tiles + [idle] * tail, np.int32).T
    return table, n_body


def _attn_masks(q_tile, k_tile):
    diag = q_tile // k_tile
    r = lax.broadcasted_iota(jnp.int32, (q_tile, k_tile), 0)
    c = lax.broadcasted_iota(jnp.int32, (q_tile, k_tile), 1)
    allowed = [((diag - 1 - t) * k_tile + c) < r for t in range(diag)]
    allowed += [jnp.ones((q_tile, k_tile), bool), jnp.zeros((q_tile, k_tile), bool)]
    return jnp.where(jnp.stack(allowed), 0.0, ATTN_MASKED_OUT).astype(F32)


def _attn_kernel(tab_ref, q_ref, k_ref, v_ref, u_ref, mask_ref, o_ref,
                 qh_ref, z_buf, sp_buf, s_buf, w_buf, acc_a, acc_b, tot_a, tot_b,
                 *, q_tile, k_tile, n_body):
    q2 = q_ref[...]
    lane = lax.broadcasted_iota(jnp.int32, q2.shape, 1)
    zero = jnp.zeros_like(q2)
    qh_ref[0] = jnp.where(lane < HEAD_DIM, q2, zero)
    qh_ref[1] = jnp.where(lane >= HEAD_DIM, q2, zero)
    for buf in (sp_buf, s_buf, w_buf, acc_a, acc_b, tot_a, tot_b):
        buf[...] = jnp.zeros_like(buf)
    z_buf[...] = jnp.full_like(z_buf, ATTN_MASKED_OUT)
    umat = u_ref[...]
    nt = (((1,), (1,)), ((), ()))
    acc_refs = (acc_a, acc_b)
    tot_refs = (tot_a, tot_b)

    def k_rows(e):
        return pl.ds(pl.multiple_of(tab_ref[TAB_KROW, e], k_tile), k_tile)

    def q_rows(e):
        return pl.ds(pl.multiple_of(tab_ref[TAB_QROW, e], q_tile), q_tile)

    def stage_qk(e, hd, zslot, slot):
        z = lax.dot_general(qh_ref[hd, q_rows(e), :], k_ref[k_rows(e), :], nt,
                            preferred_element_type=F32)
        z_buf[zslot] = z + mask_ref[tab_ref[TAB_MASK, e]]

    def stage_split(e, hd, zslot, slot):
        z = z_buf[zslot]
        sp = jnp.maximum(z, 0.0) + jnp.log(1.0 + jnp.exp2(jnp.abs(z) * NEG_LOG2E))
        sp_buf[slot] = sp.astype(BF16)

    def stage_suffix(e, hd, zslot, slot):
        s_buf[slot] = jnp.dot(sp_buf[slot], umat, preferred_element_type=F32)

    def stage_weights(e, hd, zslot, slot):
        rows = q_rows(e)
        tot = tot_refs[hd][rows, :]
        suffix = s_buf[slot]
        w_buf[slot] = jnp.exp(z_buf[zslot] - suffix - tot).astype(BF16)
        tot_refs[hd][rows, :] = tot + suffix[:, 0:1]

    def stage_pv(e, hd, zslot, slot):
        rows = q_rows(e)
        acc_refs[hd][rows, :] += jnp.dot(w_buf[slot], v_ref[k_rows(e), :],
                                         preferred_element_type=F32)

    stages = (stage_qk, stage_split, stage_suffix, stage_weights, stage_pv)
    assert len(stages) == ATTN_STAGES
    lead = (ATTN_STAGES - 1) // 2

    def body(m, carry):
        base = (ATTN_UNROLL // 2) * m + lead
        for u in range(ATTN_UNROLL):
            for k, stage in enumerate(stages):
                item = u - k
                stage(base + item // 2, item % 2, item % ATTN_UNROLL, item % ATTN_SLOTS)
        return carry

    lax.fori_loop(0, n_body, body, 0)
    o_ref[...] = jnp.where(lane < HEAD_DIM, acc_a[...], acc_b[...]).astype(o_ref.dtype)


def _stick_breaking(qkv, *, q_tile, k_tile):
    bsz, seq, d3 = qkv.shape
    d = d3 // 3
    pairs = d // V7X_LANES
    rr = lax.broadcasted_iota(jnp.int32, (k_tile, k_tile), 0)
    cc = lax.broadcasted_iota(jnp.int32, (k_tile, k_tile), 1)
    umat = (rr >= cc).astype(BF16)
    table, n_body = _attn_schedule(seq, q_tile, k_tile)
    mask = _attn_masks(q_tile, k_tile)
    kern = functools.partial(_attn_kernel, q_tile=q_tile, k_tile=k_tile, n_body=n_body)
    seq_block = lambda off: pl.BlockSpec((None, seq, V7X_LANES),
                                         lambda b, p, tab: (b, 0, off * pairs + p))
    grid_spec = pltpu.PrefetchScalarGridSpec(
        num_scalar_prefetch=1,
        grid=(bsz, pairs),
        in_specs=[seq_block(0), seq_block(1), seq_block(2),
                  _resident(umat.shape), _resident(mask.shape)],
        out_specs=seq_block(0),
        scratch_shapes=[
            pltpu.VMEM((2, seq, V7X_LANES), BF16),
            pltpu.VMEM((ATTN_UNROLL, q_tile, k_tile), F32),
            pltpu.VMEM((ATTN_SLOTS, q_tile, k_tile), BF16),
            pltpu.VMEM((ATTN_SLOTS, q_tile, k_tile), F32),
            pltpu.VMEM((ATTN_SLOTS, q_tile, k_tile), BF16),
            pltpu.VMEM((seq, V7X_LANES), F32),
            pltpu.VMEM((seq, V7X_LANES), F32),
            pltpu.VMEM((seq, 1), F32),
            pltpu.VMEM((seq, 1), F32),
        ])
    return pl.pallas_call(
        kern,
        out_shape=jax.ShapeDtypeStruct((bsz, seq, d), BF16),
        grid_spec=grid_spec,
        compiler_params=_params("arbitrary", "arbitrary"),
        name="stick_breaking",
    )(jnp.asarray(table), qkv, qkv, qkv, umat, mask)


def _proj_ln_kernel(o_ref, x_ref, w_ref, g_ref, b_ref, out_ref, *, alpha):
    mix = jnp.dot(o_ref[...], w_ref[...], preferred_element_type=F32)
    out_ref[...] = _layer_norm(alpha * x_ref[...] + mix, g_ref[...], b_ref[...])


def _proj_ln(o2d, x2d, w_o, g, b, *, alpha, row_tile):
    rows, d = x2d.shape
    kern = functools.partial(_proj_ln_kernel, alpha=alpha)
    return pl.pallas_call(
        kern,
        out_shape=jax.ShapeDtypeStruct((rows, d), F32),
        grid=(rows // row_tile,),
        in_specs=[
            pl.BlockSpec((row_tile, d), lambda i: (i, 0)),
            pl.BlockSpec((row_tile, d), lambda i: (i, 0)),
            _resident(w_o.shape),
            _resident(g.shape),
            _resident(b.shape),
        ],
        out_specs=pl.BlockSpec((row_tile, d), lambda i: (i, 0)),
        compiler_params=_params("arbitrary"),
        name="proj_ln",
    )(o2d, x2d, w_o, g, b)


def _tile(n, pref):
    return pref if n % pref == 0 else n


def kernel(x, ssm_a_re, ssm_a_im, ssm_log_dt, ssm_b_re, ssm_b_im, ssm_c_re, ssm_c_im, ssm_d,
           ssm_w_glu, ssm_w_out, sb_w_qkv, sb_w_o, ffn_w_gu, ffn_w_down,
           ln_mix_g, ln_mix_b, ln_ffn_g, ln_ffn_b):
    bsz, seq, d = x.shape
    depth = ffn_w_gu.shape[0]
    assert bsz == V7X_SUBLANES and d % V7X_LANES == 0
    alpha = (2 * depth) ** 0.25
    rows = bsz * seq
    row_tile = _tile(seq, ROW_TILE)
    ffn_tile = _tile(rows, FFN_ROW_TILE)
    vec = lambda a: a.reshape(1, d).astype(F32)

    h = x.astype(F32)
    for i in range(depth):
        j = i // 2
        if i % 2 == 0:
            bm, cm, are, aim = _s5_operators(ssm_a_re[j], ssm_a_im[j], ssm_log_dt[j],
                                             ssm_b_re[j], ssm_b_im[j], ssm_c_re[j], ssm_c_im[j])
            time_tile = _tile(seq, S5_TIME_TILE)
            y = _s5_scan(h, bm, cm, are, aim, vec(ssm_d[j]), time_tile=time_tile,
                         sub_steps=_tile(time_tile, S5_SUB_STEPS))
            h = _glu_out_ln(y.reshape(rows, d), h.reshape(rows, d), ssm_w_glu[j].astype(BF16),
                            ssm_w_out[j].astype(BF16), vec(ln_mix_g[i]), vec(ln_mix_b[i]),
                            alpha=alpha, row_tile=row_tile).reshape(bsz, seq, d)
        else:
            h2d = h.reshape(rows, d)
            qkv = _qkv_proj(h2d, sb_w_qkv[j].astype(BF16), row_tile=_tile(rows, ROW_TILE))
            o = _stick_breaking(qkv.reshape(bsz, seq, 3 * d),
                                q_tile=_tile(seq, ATTN_Q_TILE),
                                k_tile=min(ATTN_K_TILE, _tile(seq, ATTN_Q_TILE)))
            h = _proj_ln(o.reshape(rows, d), h2d, sb_w_o[j].astype(BF16),
                         vec(ln_mix_g[i]), vec(ln_mix_b[i]), alpha=alpha,
                         row_tile=_tile(rows, ROW_TILE)).reshape(bsz, seq, d)
        h = _ffn_ln(h.reshape(rows, d), ffn_w_gu[i].astype(BF16), ffn_w_down[i].astype(BF16),
                    vec(ln_ffn_g[i]), vec(ln_ffn_b[i]), alpha=alpha,
                    row_tile=ffn_tile).reshape(bsz, seq, d)
    return h
```

```python
import functools
import math

import jax
import jax.numpy as jnp
import numpy as np
from jax import lax
from jax.experimental import pallas as pl
from jax.experimental.pallas import tpu as pltpu

F32 = jnp.float32
BF16 = jnp.bfloat16

V7X_LANES = 128
V7X_SUBLANES = 8
V7X_VMEM_LIMIT_BYTES = 56 * 1024 * 1024

SSM_GROUP = 16
HEAD_DIM = 64
LN_EPS = 1e-5
LAM_RE_MAX = -1e-4

S5_TIME_TILE = 512
S5_SUB_STEPS = 128
ROW_TILE = 512
FFN_ROW_TILE = 256
ATTN_Q_TILE = 512
ATTN_K_TILE = 256


def _params(*sem):
    return pltpu.CompilerParams(dimension_semantics=sem,
                                vmem_limit_bytes=V7X_VMEM_LIMIT_BYTES)


def _resident(shape):
    zeros = (0,) * len(shape)
    return pl.BlockSpec(shape, lambda *_: zeros, pipeline_mode=pl.Buffered(1))


def _sigmoid(x):
    return 1.0 / (1.0 + jnp.exp(-x))


def _gelu_tanh(x):
    c = math.sqrt(2.0 / math.pi)
    return 0.5 * x * (1.0 + jnp.tanh(c * (x + 0.044715 * (x * x * x))))


def _layer_norm(r, g, b):
    mu = jnp.mean(r, axis=-1, keepdims=True)
    d = r - mu
    var = jnp.mean(d * d, axis=-1, keepdims=True)
    return d * lax.rsqrt(var + LN_EPS) * g + b


def _s5_kernel(x_ref, bm_ref, cm_ref, are_ref, aim_ref, d_ref, h_ref,
               state_ref, buf_a, buf_b, xs_ref, ys_ref, *, sub_steps, n_sub, batch):
    half = buf_a.shape[1] // 2
    nvr = half // V7X_LANES
    sub_rows = sub_steps * batch
    bufs = (buf_a, buf_b)

    @pl.when(pl.program_id(1) == 0)
    def _():
        state_ref[...] = jnp.zeros_like(state_ref)

    def lanes(p, imag):
        lo = (half if imag else 0) + p * V7X_LANES
        return slice(lo, lo + V7X_LANES)

    def sub(k):
        return slice(k * sub_rows, (k + 1) * sub_rows)

    def steps(k):
        return slice(k * sub_steps, (k + 1) * sub_steps)

    def project_in(k, buf):
        for b in range(batch):
            xs_ref[pl.ds(k * sub_rows + b, sub_steps, stride=batch), :] = x_ref[b, steps(k), :]
        buf[...] = jnp.dot(xs_ref[sub(k), :].astype(BF16), bm_ref[0],
                           preferred_element_type=F32)

    def project_out(k, buf):
        y = jnp.dot(buf[...].astype(BF16), cm_ref[0], preferred_element_type=F32)
        y = y + d_ref[...] * xs_ref[sub(k), :]
        ys_ref[...] = _gelu_tanh(y)
        for b in range(batch):
            h_ref[b, steps(k), :] = ys_ref[pl.ds(b, sub_steps, stride=batch), :].astype(
                h_ref.dtype)

    a_re = [are_ref[0, :, lanes(p, False)] for p in range(nvr)]
    a_im = [aim_ref[0, :, lanes(p, False)] for p in range(nvr)]
    s_re = [state_ref[:, lanes(p, False)] for p in range(nvr)]
    s_im = [state_ref[:, lanes(p, True)] for p in range(nvr)]

    def recur(buf):
        for t in range(sub_steps):
            rows = slice(t * batch, (t + 1) * batch)
            for p in range(nvr):
                v_re = a_re[p] * s_re[p] - a_im[p] * s_im[p] + buf[rows, lanes(p, False)]
                v_im = a_re[p] * s_im[p] + a_im[p] * s_re[p] + buf[rows, lanes(p, True)]
                buf[rows, lanes(p, False)] = v_re
                buf[rows, lanes(p, True)] = v_im
                s_re[p], s_im[p] = v_re, v_im

    project_in(0, bufs[0])
    for k in range(n_sub):
        cur, other = bufs[k % 2], bufs[(k + 1) % 2]
        if k >= 1:
            project_out(k - 1, other)
        if k + 1 < n_sub:
            project_in(k + 1, other)
        recur(cur)
    project_out(n_sub - 1, bufs[(n_sub - 1) % 2])

    for p in range(nvr):
        state_ref[:, lanes(p, False)] = s_re[p]
        state_ref[:, lanes(p, True)] = s_im[p]


def _s5_operators(a_re, a_im, log_dt, b_re, b_im, c_re, c_im):
    g, p = a_re.shape
    h = b_re.shape[-1]
    gl = V7X_LANES // h
    nj = g // gl
    dt = jnp.exp(log_dt.astype(F32))[:, None]
    lam_re = jnp.minimum(a_re.astype(F32), LAM_RE_MAX)
    lam_im = a_im.astype(F32)
    mag = jnp.exp(lam_re * dt)
    lbar_re = mag * jnp.cos(lam_im * dt)
    lbar_im = mag * jnp.sin(lam_im * dt)
    den = lam_re * lam_re + lam_im * lam_im
    num_re = lbar_re - 1.0
    num_im = lbar_im
    coef_re = (num_re * lam_re + num_im * lam_im) / den
    coef_im = (num_im * lam_re - num_re * lam_im) / den
    cr, ci = coef_re[..., None], coef_im[..., None]
    br, bi = b_re.astype(F32), b_im.astype(F32)
    bb_re = cr * br - ci * bi
    bb_im = cr * bi + ci * br
    eye = jnp.eye(gl, dtype=F32)

    def b_blocks(bb):
        t = bb.transpose(0, 2, 1).reshape(nj, gl, h, p)
        return jnp.einsum('jahp,ab->jahbp', t, eye).reshape(nj, gl * h, gl * p)

    def c_blocks(c):
        t = c.astype(F32).reshape(nj, gl, h, p)
        return jnp.einsum('jahp,ab->japbh', t, eye).reshape(nj, gl * p, gl * h)

    bm = jnp.concatenate([b_blocks(bb_re), b_blocks(bb_im)], axis=2).astype(BF16)
    cm = jnp.concatenate([c_blocks(c_re), -c_blocks(c_im)], axis=1).astype(BF16)
    are = jnp.broadcast_to(lbar_re.reshape(nj, 1, gl * p), (nj, V7X_SUBLANES, gl * p))
    aim = jnp.broadcast_to(lbar_im.reshape(nj, 1, gl * p), (nj, V7X_SUBLANES, gl * p))
    return bm, cm, are, aim


def _s5_scan(x, bm, cm, are, aim, d_skip, *, time_tile, sub_steps):
    batch, seq, d = x.shape
    nj = d // V7X_LANES
    nstate = bm.shape[2]
    sub_rows = sub_steps * batch
    kern = functools.partial(_s5_kernel, sub_steps=sub_steps, n_sub=time_tile // sub_steps,
                             batch=batch)
    io_block = pl.BlockSpec((batch, time_tile, V7X_LANES), lambda j, c: (0, c, j))
    return pl.pallas_call(
        kern,
        out_shape=jax.ShapeDtypeStruct((batch, seq, d), BF16),
        grid=(nj, seq // time_tile),
        in_specs=[
            io_block,
            pl.BlockSpec((1, V7X_LANES, nstate), lambda j, c: (j, 0, 0)),
            pl.BlockSpec((1, nstate, V7X_LANES), lambda j, c: (j, 0, 0)),
            pl.BlockSpec((1, V7X_SUBLANES, nstate // 2), lambda j, c: (j, 0, 0)),
            pl.BlockSpec((1, V7X_SUBLANES, nstate // 2), lambda j, c: (j, 0, 0)),
            pl.BlockSpec((1, V7X_LANES), lambda j, c: (0, j)),
        ],
        out_specs=io_block,
        scratch_shapes=[pltpu.VMEM((batch, nstate), F32),
                        pltpu.VMEM((sub_rows, nstate), F32),
                        pltpu.VMEM((sub_rows, nstate), F32),
                        pltpu.VMEM((time_tile * batch, V7X_LANES), F32),
                        pltpu.VMEM((sub_rows, V7X_LANES), F32)],
        compiler_params=_params("arbitrary", "arbitrary"),
        name="s5_scan",
    )(x, bm, cm, are, aim, d_skip)


def _ffn_ln_value(x, wgu_ref, wdown_ref, g_ref, b_ref, alpha):
    f = wdown_ref.shape[0]
    gu = jnp.dot(x.astype(BF16), wgu_ref[...], preferred_element_type=F32)
    gate = gu[:, :f]
    act = gate * _sigmoid(gate) * gu[:, f:]
    y = jnp.dot(act.astype(BF16), wdown_ref[...], preferred_element_type=F32)
    return _layer_norm(alpha * x + y, g_ref[...], b_ref[...])


def _tail_ffn_kernel(a_ref, x_ref, *refs, alpha, glu):
    if glu:
        wglu_ref, refs = refs[0], refs[1:]
        d = x_ref.shape[-1]
        vg = jnp.dot(a_ref[...], wglu_ref[...], preferred_element_type=F32)
        a = (vg[:, :d] * _sigmoid(vg[:, d:])).astype(BF16)
    else:
        a = a_ref[...]
    wout_ref, g1_ref, b1_ref, wgu_ref, wdown_ref, g2_ref, b2_ref, o_ref = refs
    mix = jnp.dot(a, wout_ref[...], preferred_element_type=F32)
    h = _layer_norm(alpha * x_ref[...] + mix, g1_ref[...], b1_ref[...])
    o_ref[...] = _ffn_ln_value(h, wgu_ref, wdown_ref, g2_ref, b2_ref, alpha)


def _tail_ffn(a2d, x2d, mixer_weights, ln1, w_gu, w_down, ln2, *, alpha, row_tile):
    rows, d = x2d.shape
    consts = (*mixer_weights, *ln1, w_gu, w_down, *ln2)
    kern = functools.partial(_tail_ffn_kernel, alpha=alpha, glu=len(mixer_weights) == 2)
    return pl.pallas_call(
        kern,
        out_shape=jax.ShapeDtypeStruct((rows, d), F32),
        grid=(rows // row_tile,),
        in_specs=[pl.BlockSpec((row_tile, d), lambda i: (i, 0)),
                  pl.BlockSpec((row_tile, d), lambda i: (i, 0))]
                 + [_resident(c.shape) for c in consts],
        out_specs=pl.BlockSpec((row_tile, d), lambda i: (i, 0)),
        compiler_params=_params("arbitrary"),
        name="tail_ffn",
    )(a2d, x2d, *consts)


def _qkv_kernel(x_ref, w_ref, o_ref, *, d, scale):
    acc = jnp.dot(x_ref[...].astype(BF16), w_ref[...], preferred_element_type=F32)
    o_ref[:, :d] = (acc[:, :d] * scale).astype(o_ref.dtype)
    o_ref[:, d:] = acc[:, d:].astype(o_ref.dtype)


def _qkv_proj(x2d, w_qkv, *, row_tile):
    rows, d = x2d.shape
    kern = functools.partial(_qkv_kernel, d=d, scale=HEAD_DIM ** -0.5)
    return pl.pallas_call(
        kern,
        out_shape=jax.ShapeDtypeStruct((rows, 3 * d), BF16),
        grid=(rows // row_tile,),
        in_specs=[pl.BlockSpec((row_tile, d), lambda i: (i, 0)),
                  _resident(w_qkv.shape)],
        out_specs=pl.BlockSpec((row_tile, 3 * d), lambda i: (i, 0)),
        compiler_params=_params("arbitrary"),
        name="qkv_proj",
    )(x2d, w_qkv)


ATTN_STAGES = 3
ATTN_UNROLL = 8
ATTN_SLOTS = 4
ATTN_MASKED_OUT = -1e30
NEG_LOG2E = -1.4426950408889634
TAB_QROW, TAB_KROW, TAB_MASK = range(3)


def _attn_schedule(seq, q_tile, k_tile):
    diag = q_tile // k_tile
    tiles = []
    for i in range(seq // q_tile):
        n_kv = (i + 1) * diag
        for jj in range(n_kv):
            tiles.append((i * q_tile, (n_kv - 1 - jj) * k_tile, min(jj, diag)))
    idle = (0, 0, diag + 1)
    lead = (ATTN_STAGES - 1) // 2
    tiles_per_body = ATTN_UNROLL // 2
    n_body = -(-(2 * len(tiles) + ATTN_STAGES - 1) // ATTN_UNROLL)
    tail = tiles_per_body * n_body - len(tiles)
    table = np.asarray([idle] * lead + tiles + [idle] * tail, np.int32).T
    return table, n_body


def _attn_masks(q_tile, k_tile):
    diag = q_tile // k_tile
    r = lax.broadcasted_iota(jnp.int32, (q_tile, k_tile), 0)
    c = lax.broadcasted_iota(jnp.int32, (q_tile, k_tile), 1)
    allowed = [((diag - 1 - t) * k_tile + c) < r for t in range(diag)]
    allowed += [jnp.ones((q_tile, k_tile), bool), jnp.zeros((q_tile, k_tile), bool)]
    return jnp.where(jnp.stack(allowed), 0.0, ATTN_MASKED_OUT).astype(F32)


def _attn_kernel(tab_ref, q_ref, k_ref, v_ref, u_ref, mask_ref, o_ref,
                 qh_ref, z_buf, sp_buf, w_buf, acc_a, acc_b, tot_a, tot_b,
                 *, q_tile, k_tile, n_body):
    q2 = q_ref[...]
    lane = lax.broadcasted_iota(jnp.int32, q2.shape, 1)
    zero = jnp.zeros_like(q2)
    qh_ref[0] = jnp.where(lane < HEAD_DIM, q2, zero)
    qh_ref[1] = jnp.where(lane >= HEAD_DIM, q2, zero)
    for buf in (sp_buf, w_buf, acc_a, acc_b, tot_a, tot_b):
        buf[...] = jnp.zeros_like(buf)
    z_buf[...] = jnp.full_like(z_buf, ATTN_MASKED_OUT)
    umat = u_ref[...]
    nt = (((1,), (1,)), ((), ()))
    acc_refs = (acc_a, acc_b)
    tot_refs = (tot_a, tot_b)

    def k_rows(e):
        return pl.ds(pl.multiple_of(tab_ref[TAB_KROW, e], k_tile), k_tile)

    def q_rows(e):
        return pl.ds(pl.multiple_of(tab_ref[TAB_QROW, e], q_tile), q_tile)

    def stage_scores(e, hd, slot):
        z = lax.dot_general(qh_ref[hd, q_rows(e), :], k_ref[k_rows(e), :], nt,
                            preferred_element_type=F32)
        z = z + mask_ref[tab_ref[TAB_MASK, e]]
        z_buf[slot] = z
        sp = jnp.maximum(z, 0.0) + jnp.log(1.0 + jnp.exp2(jnp.abs(z) * NEG_LOG2E))
        sp_buf[slot] = sp.astype(BF16)

    def stage_weights(e, hd, slot):
        rows = q_rows(e)
        tot = tot_refs[hd][rows, :]
        suffix = jnp.dot(sp_buf[slot], umat, preferred_element_type=F32)
        w_buf[slot] = jnp.exp(z_buf[slot] - suffix - tot).astype(BF16)
        tot_refs[hd][rows, :] = tot + suffix[:, 0:1]

    def stage_pv(e, hd, slot):
        rows = q_rows(e)
        acc_refs[hd][rows, :] += jnp.dot(w_buf[slot], v_ref[k_rows(e), :],
                                         preferred_element_type=F32)

    stages = (stage_scores, stage_weights, stage_pv)
    assert len(stages) == ATTN_STAGES
    lead = (ATTN_STAGES - 1) // 2

    def body(m, carry):
        base = (ATTN_UNROLL // 2) * m + lead
        for u in range(ATTN_UNROLL):
            for k, stage in enumerate(stages):
                item = u - k
                stage(base + item // 2, item % 2, item % ATTN_SLOTS)
        return carry

    lax.fori_loop(0, n_body, body, 0)
    o_ref[...] = jnp.where(lane < HEAD_DIM, acc_a[...], acc_b[...]).astype(o_ref.dtype)


def _stick_breaking(qkv, *, q_tile, k_tile):
    bsz, seq, d3 = qkv.shape
    d = d3 // 3
    pairs = d // V7X_LANES
    rr = lax.broadcasted_iota(jnp.int32, (k_tile, k_tile), 0)
    cc = lax.broadcasted_iota(jnp.int32, (k_tile, k_tile), 1)
    umat = (rr >= cc).astype(BF16)
    table, n_body = _attn_schedule(seq, q_tile, k_tile)
    mask = _attn_masks(q_tile, k_tile)
    kern = functools.partial(_attn_kernel, q_tile=q_tile, k_tile=k_tile, n_body=n_body)
    seq_block = lambda off: pl.BlockSpec((None, seq, V7X_LANES),
                                         lambda b, p, tab: (b, 0, off * pairs + p))
    grid_spec = pltpu.PrefetchScalarGridSpec(
        num_scalar_prefetch=1,
        grid=(bsz, pairs),
        in_specs=[seq_block(0), seq_block(1), seq_block(2),
                  _resident(umat.shape), _resident(mask.shape)],
        out_specs=seq_block(0),
        scratch_shapes=[
            pltpu.VMEM((2, seq, V7X_LANES), BF16),
            pltpu.VMEM((ATTN_SLOTS, q_tile, k_tile), F32),
            pltpu.VMEM((ATTN_SLOTS, q_tile, k_tile), BF16),
            pltpu.VMEM((ATTN_SLOTS, q_tile, k_tile), BF16),
            pltpu.VMEM((seq, V7X_LANES), F32),
            pltpu.VMEM((seq, V7X_LANES), F32),
            pltpu.VMEM((seq, 1), F32),
            pltpu.VMEM((seq, 1), F32),
        ])
    return pl.pallas_call(
        kern,
        out_shape=jax.ShapeDtypeStruct((bsz, seq, d), BF16),
        grid_spec=grid_spec,
        compiler_params=_params("arbitrary", "arbitrary"),
        name="stick_breaking",
    )(jnp.asarray(table), qkv, qkv, qkv, umat, mask)


def _tile(n, pref):
    return pref if n % pref == 0 else n


def kernel(x, ssm_a_re, ssm_a_im, ssm_log_dt, ssm_b_re, ssm_b_im, ssm_c_re, ssm_c_im, ssm_d,
           ssm_w_glu, ssm_w_out, sb_w_qkv, sb_w_o, ffn_w_gu, ffn_w_down,
           ln_mix_g, ln_mix_b, ln_ffn_g, ln_ffn_b):
    bsz, seq, d = x.shape
    depth = ffn_w_gu.shape[0]
    assert bsz == V7X_SUBLANES and d % V7X_LANES == 0
    alpha = (2 * depth) ** 0.25
    rows = bsz * seq
    ffn_tile = _tile(rows, FFN_ROW_TILE)
    vec = lambda a: a.reshape(1, d).astype(F32)

    h = x.astype(F32)
    for i in range(depth):
        j = i // 2
        h2d = h.reshape(rows, d)
        if i % 2 == 0:
            bm, cm, are, aim = _s5_operators(ssm_a_re[j], ssm_a_im[j], ssm_log_dt[j],
                                             ssm_b_re[j], ssm_b_im[j], ssm_c_re[j], ssm_c_im[j])
            time_tile = _tile(seq, S5_TIME_TILE)
            a = _s5_scan(h, bm, cm, are, aim, vec(ssm_d[j]), time_tile=time_tile,
                         sub_steps=_tile(time_tile, S5_SUB_STEPS))
            mixer_weights = (ssm_w_glu[j].astype(BF16), ssm_w_out[j].astype(BF16))
        else:
            qkv = _qkv_proj(h2d, sb_w_qkv[j].astype(BF16), row_tile=_tile(rows, ROW_TILE))
            a = _stick_breaking(qkv.reshape(bsz, seq, 3 * d),
                                q_tile=_tile(seq, ATTN_Q_TILE),
                                k_tile=min(ATTN_K_TILE, _tile(seq, ATTN_Q_TILE)))
            mixer_weights = (sb_w_o[j].astype(BF16),)
        h = _tail_ffn(a.reshape(rows, d), h2d, mixer_weights,
                      (vec(ln_mix_g[i]), vec(ln_mix_b[i])),
                      ffn_w_gu[i].astype(BF16), ffn_w_down[i].astype(BF16),
                      (vec(ln_ffn_g[i]), vec(ln_ffn_b[i])),
                      alpha=alpha, row_tile=ffn_tile).reshape(bsz, seq, d)
    return h
```

```python
import functools
import math

import jax
import jax.numpy as jnp
import numpy as np
from jax import lax
from jax.experimental import pallas as pl
from jax.experimental.pallas import tpu as pltpu

F32 = jnp.float32
BF16 = jnp.bfloat16

V7X_LANES = 128
V7X_SUBLANES = 8
V7X_VMEM_LIMIT_BYTES = 56 * 1024 * 1024

SSM_GROUP = 16
HEAD_DIM = 64
LN_EPS = 1e-5
LAM_RE_MAX = -1e-4

S5_TIME_TILE = 512
S5_SUB_STEPS = 128
ROW_TILE = 512
FFN_ROW_TILE = 256
ATTN_Q_TILE = 512
ATTN_K_TILE = 256


def _params(*sem):
    return pltpu.CompilerParams(dimension_semantics=sem,
                                vmem_limit_bytes=V7X_VMEM_LIMIT_BYTES)


def _resident(shape):
    zeros = (0,) * len(shape)
    return pl.BlockSpec(shape, lambda *_: zeros, pipeline_mode=pl.Buffered(1))


def _sigmoid(x):
    return 1.0 / (1.0 + jnp.exp(-x))


def _gelu_tanh(x):
    c = math.sqrt(2.0 / math.pi)
    return 0.5 * x * (1.0 + jnp.tanh(c * (x + 0.044715 * (x * x * x))))


def _layer_norm(r, g, b):
    mu = jnp.mean(r, axis=-1, keepdims=True)
    d = r - mu
    var = jnp.mean(d * d, axis=-1, keepdims=True)
    return d * lax.rsqrt(var + LN_EPS) * g + b


def _s5_kernel(x_ref, bm_ref, cm_ref, are_ref, aim_ref, d_ref, h_ref,
               state_ref, buf_a, buf_b, xs_ref, ys_ref, *, sub_steps, n_sub, batch):
    half = buf_a.shape[1] // 2
    nvr = half // V7X_LANES
    sub_rows = sub_steps * batch
    bufs = (buf_a, buf_b)

    @pl.when(pl.program_id(1) == 0)
    def _():
        state_ref[...] = jnp.zeros_like(state_ref)

    def lanes(p, imag):
        lo = (half if imag else 0) + p * V7X_LANES
        return slice(lo, lo + V7X_LANES)

    def sub(k):
        return slice(k * sub_rows, (k + 1) * sub_rows)

    def steps(k):
        return slice(k * sub_steps, (k + 1) * sub_steps)

    def project_in(k, buf):
        for b in range(batch):
            xs_ref[pl.ds(k * sub_rows + b, sub_steps, stride=batch), :] = x_ref[b, steps(k), :]
        buf[...] = jnp.dot(xs_ref[sub(k), :].astype(BF16), bm_ref[0],
                           preferred_element_type=F32)

    def project_out(k, buf):
        y = jnp.dot(buf[...].astype(BF16), cm_ref[0], preferred_element_type=F32)
        y = y + d_ref[...] * xs_ref[sub(k), :]
        ys_ref[...] = _gelu_tanh(y)
        for b in range(batch):
            h_ref[b, steps(k), :] = ys_ref[pl.ds(b, sub_steps, stride=batch), :].astype(
                h_ref.dtype)

    a_re = [are_ref[0, :, lanes(p, False)] for p in range(nvr)]
    a_im = [aim_ref[0, :, lanes(p, False)] for p in range(nvr)]
    s_re = [state_ref[:, lanes(p, False)] for p in range(nvr)]
    s_im = [state_ref[:, lanes(p, True)] for p in range(nvr)]

    def recur(buf):
        for t in range(sub_steps):
            rows = slice(t * batch, (t + 1) * batch)
            for p in range(nvr):
                v_re = a_re[p] * s_re[p] - a_im[p] * s_im[p] + buf[rows, lanes(p, False)]
                v_im = a_re[p] * s_im[p] + a_im[p] * s_re[p] + buf[rows, lanes(p, True)]
                buf[rows, lanes(p, False)] = v_re
                buf[rows, lanes(p, True)] = v_im
                s_re[p], s_im[p] = v_re, v_im

    project_in(0, bufs[0])
    for k in range(n_sub):
        cur, other = bufs[k % 2], bufs[(k + 1) % 2]
        if k >= 1:
            project_out(k - 1, other)
        if k + 1 < n_sub:
            project_in(k + 1, other)
        recur(cur)
    project_out(n_sub - 1, bufs[(n_sub - 1) % 2])

    for p in range(nvr):
        state_ref[:, lanes(p, False)] = s_re[p]
        state_ref[:, lanes(p, True)] = s_im[p]


def _s5_operators(a_re, a_im, log_dt, b_re, b_im, c_re, c_im):
    g, p = a_re.shape
    h = b_re.shape[-1]
    gl = V7X_LANES // h
    nj = g // gl
    dt = jnp.exp(log_dt.astype(F32))[:, None]
    lam_re = jnp.minimum(a_re.astype(F32), LAM_RE_MAX)
    lam_im = a_im.astype(F32)
    mag = jnp.exp(lam_re * dt)
    lbar_re = mag * jnp.cos(lam_im * dt)
    lbar_im = mag * jnp.sin(lam_im * dt)
    den = lam_re * lam_re + lam_im * lam_im
    num_re = lbar_re - 1.0
    num_im = lbar_im
    coef_re = (num_re * lam_re + num_im * lam_im) / den
    coef_im = (num_im * lam_re - num_re * lam_im) / den
    cr, ci = coef_re[..., None], coef_im[..., None]
    br, bi = b_re.astype(F32), b_im.astype(F32)
    bb_re = cr * br - ci * bi
    bb_im = cr * bi + ci * br
    eye = jnp.eye(gl, dtype=F32)

    def b_blocks(bb):
        t = bb.transpose(0, 2, 1).reshape(nj, gl, h, p)
        return jnp.einsum('jahp,ab->jahbp', t, eye).reshape(nj, gl * h, gl * p)

    def c_blocks(c):
        t = c.astype(F32).reshape(nj, gl, h, p)
        return jnp.einsum('jahp,ab->japbh', t, eye).reshape(nj, gl * p, gl * h)

    bm = jnp.concatenate([b_blocks(bb_re), b_blocks(bb_im)], axis=2).astype(BF16)
    cm = jnp.concatenate([c_blocks(c_re), -c_blocks(c_im)], axis=1).astype(BF16)
    are = jnp.broadcast_to(lbar_re.reshape(nj, 1, gl * p), (nj, V7X_SUBLANES, gl * p))
    aim = jnp.broadcast_to(lbar_im.reshape(nj, 1, gl * p), (nj, V7X_SUBLANES, gl * p))
    return bm, cm, are, aim


def _s5_scan(x, bm, cm, are, aim, d_skip, *, time_tile, sub_steps):
    batch, seq, d = x.shape
    nj = d // V7X_LANES
    nstate = bm.shape[2]
    sub_rows = sub_steps * batch
    kern = functools.partial(_s5_kernel, sub_steps=sub_steps, n_sub=time_tile // sub_steps,
                             batch=batch)
    io_block = pl.BlockSpec((batch, time_tile, V7X_LANES), lambda j, c: (0, c, j))
    return pl.pallas_call(
        kern,
        out_shape=jax.ShapeDtypeStruct((batch, seq, d), BF16),
        grid=(nj, seq // time_tile),
        in_specs=[
            io_block,
            pl.BlockSpec((1, V7X_LANES, nstate), lambda j, c: (j, 0, 0)),
            pl.BlockSpec((1, nstate, V7X_LANES), lambda j, c: (j, 0, 0)),
            pl.BlockSpec((1, V7X_SUBLANES, nstate // 2), lambda j, c: (j, 0, 0)),
            pl.BlockSpec((1, V7X_SUBLANES, nstate // 2), lambda j, c: (j, 0, 0)),
            pl.BlockSpec((1, V7X_LANES), lambda j, c: (0, j)),
        ],
        out_specs=io_block,
        scratch_shapes=[pltpu.VMEM((batch, nstate), F32),
                        pltpu.VMEM((sub_rows, nstate), F32),
                        pltpu.VMEM((sub_rows, nstate), F32),
                        pltpu.VMEM((time_tile * batch, V7X_LANES), F32),
                        pltpu.VMEM((sub_rows, V7X_LANES), F32)],
        compiler_params=_params("arbitrary", "arbitrary"),
        name="s5_scan",
    )(x, bm, cm, are, aim, d_skip)


def _ffn_ln_value(x, wgu_ref, wdown_ref, g_ref, b_ref, alpha):
    f = wdown_ref.shape[0]
    gu = jnp.dot(x.astype(BF16), wgu_ref[...], preferred_element_type=F32)
    gate = gu[:, :f]
    act = gate * _sigmoid(gate) * gu[:, f:]
    y = jnp.dot(act.astype(BF16), wdown_ref[...], preferred_element_type=F32)
    return _layer_norm(alpha * x + y, g_ref[...], b_ref[...])


def _tail_ffn_kernel(a_ref, x_ref, *refs, alpha, glu):
    if glu:
        wglu_ref, refs = refs[0], refs[1:]
        d = x_ref.shape[-1]
        vg = jnp.dot(a_ref[...], wglu_ref[...], preferred_element_type=F32)
        a = (vg[:, :d] * _sigmoid(vg[:, d:])).astype(BF16)
    else:
        a = a_ref[...]
    wout_ref, g1_ref, b1_ref, wgu_ref, wdown_ref, g2_ref, b2_ref, o_ref = refs
    mix = jnp.dot(a, wout_ref[...], preferred_element_type=F32)
    h = _layer_norm(alpha * x_ref[...] + mix, g1_ref[...], b1_ref[...])
    o_ref[...] = _ffn_ln_value(h, wgu_ref, wdown_ref, g2_ref, b2_ref, alpha)


def _tail_ffn(a2d, x2d, mixer_weights, ln1, w_gu, w_down, ln2, *, alpha, row_tile):
    rows, d = x2d.shape
    consts = (*mixer_weights, *ln1, w_gu, w_down, *ln2)
    kern = functools.partial(_tail_ffn_kernel, alpha=alpha, glu=len(mixer_weights) == 2)
    return pl.pallas_call(
        kern,
        out_shape=jax.ShapeDtypeStruct((rows, d), F32),
        grid=(rows // row_tile,),
        in_specs=[pl.BlockSpec((row_tile, d), lambda i: (i, 0)),
                  pl.BlockSpec((row_tile, d), lambda i: (i, 0))]
                 + [_resident(c.shape) for c in consts],
        out_specs=pl.BlockSpec((row_tile, d), lambda i: (i, 0)),
        compiler_params=_params("arbitrary"),
        name="tail_ffn",
    )(a2d, x2d, *consts)


def _qkv_kernel(x_ref, w_ref, o_ref, *, d, scale):
    acc = jnp.dot(x_ref[...].astype(BF16), w_ref[...], preferred_element_type=F32)
    o_ref[:, :d] = (acc[:, :d] * scale).astype(o_ref.dtype)
    o_ref[:, d:] = acc[:, d:].astype(o_ref.dtype)


def _qkv_proj(x2d, w_qkv, *, row_tile):
    rows, d = x2d.shape
    kern = functools.partial(_qkv_kernel, d=d, scale=HEAD_DIM ** -0.5)
    return pl.pallas_call(
        kern,
        out_shape=jax.ShapeDtypeStruct((rows, 3 * d), BF16),
        grid=(rows // row_tile,),
        in_specs=[pl.BlockSpec((row_tile, d), lambda i: (i, 0)),
                  _resident(w_qkv.shape)],
        out_specs=pl.BlockSpec((row_tile, 3 * d), lambda i: (i, 0)),
        compiler_params=_params("arbitrary"),
        name="qkv_proj",
    )(x2d, w_qkv)


ATTN_STAGES = 3
ATTN_UNROLL = 8
ATTN_SLOTS = 4
ATTN_MASKED_OUT = -1e30
NEG_LOG2E = -1.4426950408889634
TAB_QROW, TAB_KROW, TAB_MASK = range(3)


def _attn_schedule(seq, q_tile, k_tile):
    diag = q_tile // k_tile
    per_body = ATTN_UNROLL // 2
    masked, below = [], []
    for i in range(seq // q_tile):
        n_kv = (i + 1) * diag
        for jj in range(n_kv):
            tile = (i * q_tile, (n_kv - 1 - jj) * k_tile, min(jj, diag))
            (masked if jj < diag else below).append(tile)
    spill = len(below) % per_body
    masked, below = masked + below[:spill], below[spill:]
    idle = (0, 0, diag + 1)
    masked += [idle] * (-len(masked) % per_body)
    lead = (ATTN_STAGES - 1) // 2

    def dead_rows(tiles, pos):
        kinds = {t[2] for t in tiles[pos::per_body]} - {diag + 1}
        return min(((diag - 1 - kind) * k_tile if kind < diag else 0 for kind in kinds),
                   default=0)

    cols, phases = [], []
    for tiles, use_mask in ((masked, True), (below, False)):
        if tiles:
            skip = tuple(dead_rows(tiles, pos) if use_mask else 0 for pos in range(per_body))
            phases.append((len(cols) + lead, len(tiles) // per_body, use_mask, skip))
            cols += [idle] * lead + tiles
    return np.asarray(cols, np.int32).T, tuple(phases)


def _attn_masks(q_tile, k_tile):
    diag = q_tile // k_tile
    r = lax.broadcasted_iota(jnp.int32, (q_tile, k_tile), 0)
    c = lax.broadcasted_iota(jnp.int32, (q_tile, k_tile), 1)
    allowed = [((diag - 1 - t) * k_tile + c) < r for t in range(diag)]
    allowed += [jnp.ones((q_tile, k_tile), bool), jnp.zeros((q_tile, k_tile), bool)]
    return jnp.where(jnp.stack(allowed), 0.0, ATTN_MASKED_OUT).astype(F32)


def _attn_kernel(tab_ref, q_ref, k_ref, v_ref, u_ref, mask_ref, o_ref,
                 qh_ref, z_buf, sp_buf, w_buf, acc_a, acc_b, tot_a, tot_b,
                 *, q_tile, k_tile, phases):
    q2 = q_ref[...]
    lane = lax.broadcasted_iota(jnp.int32, q2.shape, 1)
    zero = jnp.zeros_like(q2)
    qh_ref[0] = jnp.where(lane < HEAD_DIM, q2, zero)
    qh_ref[1] = jnp.where(lane >= HEAD_DIM, q2, zero)
    for buf in (acc_a, acc_b, tot_a, tot_b):
        buf[...] = jnp.zeros_like(buf)
    umat = u_ref[...]
    nt = (((1,), (1,)), ((), ()))
    acc_refs = (acc_a, acc_b)
    tot_refs = (tot_a, tot_b)

    def k_rows(e):
        return pl.ds(pl.multiple_of(tab_ref[TAB_KROW, e], k_tile), k_tile)

    def q_rows(e, r0):
        start = pl.multiple_of(tab_ref[TAB_QROW, e] + r0, k_tile)
        return pl.ds(start, q_tile - r0)

    def stage_scores(e, hd, slot, r0, use_mask):
        z = lax.dot_general(qh_ref[hd, q_rows(e, r0), :], k_ref[k_rows(e), :], nt,
                            preferred_element_type=F32)
        if use_mask:
            z = z + mask_ref[tab_ref[TAB_MASK, e], r0:, :]
        z_buf[slot, r0:, :] = z
        sp = jnp.maximum(z, 0.0) + jnp.log(1.0 + jnp.exp2(jnp.abs(z) * NEG_LOG2E))
        sp_buf[slot, r0:, :] = sp.astype(BF16)

    def stage_weights(e, hd, slot, r0):
        rows = q_rows(e, r0)
        tot = tot_refs[hd][rows, :]
        suffix = jnp.dot(sp_buf[slot, r0:, :], umat, preferred_element_type=F32)
        w_buf[slot, r0:, :] = jnp.exp(z_buf[slot, r0:, :] - suffix - tot).astype(BF16)
        tot_refs[hd][rows, :] = tot + suffix[:, 0:1]

    def stage_pv(e, hd, slot, r0):
        rows = q_rows(e, r0)
        acc_refs[hd][rows, :] += jnp.dot(w_buf[slot, r0:, :], v_ref[k_rows(e), :],
                                         preferred_element_type=F32)

    def run_phase(slot_shift, first_col, n_body, use_mask, skip):
        stages = (functools.partial(stage_scores, use_mask=use_mask), stage_weights, stage_pv)
        assert len(stages) == ATTN_STAGES

        def slot_of(item):
            return (item + slot_shift) % ATTN_SLOTS

        def run(stage, base, item):
            stage(base + item // 2, item % 2, slot_of(item), skip[(item // 2) % len(skip)])

        for item in range(1 - ATTN_STAGES, 0):
            w_buf[slot_of(item)] = jnp.zeros((q_tile, k_tile), BF16)
        for item in range(2 - ATTN_STAGES, 0):
            z_buf[slot_of(item)] = jnp.full((q_tile, k_tile), ATTN_MASKED_OUT, F32)
            sp_buf[slot_of(item)] = jnp.zeros((q_tile, k_tile), BF16)

        def body(m, carry):
            base = (ATTN_UNROLL // 2) * m + first_col
            for u in range(ATTN_UNROLL):
                for k, stage in enumerate(stages):
                    run(stage, base, u - k)
            return carry

        lax.fori_loop(0, n_body, body, 0)
        for t in range(ATTN_STAGES - 1):
            for k in range(t + 1, ATTN_STAGES):
                run(stages[k], first_col, ATTN_UNROLL * n_body + t - k)

    for p, phase in enumerate(phases):
        run_phase((ATTN_STAGES - 1) * (p % 2), *phase)
    o_ref[...] = jnp.where(lane < HEAD_DIM, acc_a[...], acc_b[...]).astype(o_ref.dtype)


def _stick_breaking(qkv, *, q_tile, k_tile):
    bsz, seq, d3 = qkv.shape
    d = d3 // 3
    pairs = d // V7X_LANES
    rr = lax.broadcasted_iota(jnp.int32, (k_tile, k_tile), 0)
    cc = lax.broadcasted_iota(jnp.int32, (k_tile, k_tile), 1)
    umat = (rr >= cc).astype(BF16)
    table, phases = _attn_schedule(seq, q_tile, k_tile)
    mask = _attn_masks(q_tile, k_tile)
    kern = functools.partial(_attn_kernel, q_tile=q_tile, k_tile=k_tile, phases=phases)
    seq_block = lambda off: pl.BlockSpec((None, seq, V7X_LANES),
                                         lambda b, p, tab: (b, 0, off * pairs + p))
    grid_spec = pltpu.PrefetchScalarGridSpec(
        num_scalar_prefetch=1,
        grid=(bsz, pairs),
        in_specs=[seq_block(0), seq_block(1), seq_block(2),
                  _resident(umat.shape), _resident(mask.shape)],
        out_specs=seq_block(0),
        scratch_shapes=[
            pltpu.VMEM((2, seq, V7X_LANES), BF16),
            pltpu.VMEM((ATTN_SLOTS, q_tile, k_tile), F32),
            pltpu.VMEM((ATTN_SLOTS, q_tile, k_tile), BF16),
            pltpu.VMEM((ATTN_SLOTS, q_tile, k_tile), BF16),
            pltpu.VMEM((seq, V7X_LANES), F32),
            pltpu.VMEM((seq, V7X_LANES), F32),
            pltpu.VMEM((seq, 1), F32),
            pltpu.VMEM((seq, 1), F32),
        ])
    return pl.pallas_call(
        kern,
        out_shape=jax.ShapeDtypeStruct((bsz, seq, d), BF16),
        grid_spec=grid_spec,
        compiler_params=_params("arbitrary", "arbitrary"),
        name="stick_breaking",
    )(jnp.asarray(table), qkv, qkv, qkv, umat, mask)


def _tile(n, pref):
    return pref if n % pref == 0 else n


def kernel(x, ssm_a_re, ssm_a_im, ssm_log_dt, ssm_b_re, ssm_b_im, ssm_c_re, ssm_c_im, ssm_d,
           ssm_w_glu, ssm_w_out, sb_w_qkv, sb_w_o, ffn_w_gu, ffn_w_down,
           ln_mix_g, ln_mix_b, ln_ffn_g, ln_ffn_b):
    bsz, seq, d = x.shape
    depth = ffn_w_gu.shape[0]
    assert bsz == V7X_SUBLANES and d % V7X_LANES == 0
    alpha = (2 * depth) ** 0.25
    rows = bsz * seq
    ffn_tile = _tile(rows, FFN_ROW_TILE)
    vec = lambda a: a.reshape(1, d).astype(F32)

    h = x.astype(F32)
    for i in range(depth):
        j = i // 2
        h2d = h.reshape(rows, d)
        if i % 2 == 0:
            bm, cm, are, aim = _s5_operators(ssm_a_re[j], ssm_a_im[j], ssm_log_dt[j],
                                             ssm_b_re[j], ssm_b_im[j], ssm_c_re[j], ssm_c_im[j])
            time_tile = _tile(seq, S5_TIME_TILE)
            a = _s5_scan(h, bm, cm, are, aim, vec(ssm_d[j]), time_tile=time_tile,
                         sub_steps=_tile(time_tile, S5_SUB_STEPS))
            mixer_weights = (ssm_w_glu[j].astype(BF16), ssm_w_out[j].astype(BF16))
        else:
            qkv = _qkv_proj(h2d, sb_w_qkv[j].astype(BF16), row_tile=_tile(rows, ROW_TILE))
            a = _stick_breaking(qkv.reshape(bsz, seq, 3 * d),
                                q_tile=_tile(seq, ATTN_Q_TILE),
                                k_tile=min(ATTN_K_TILE, _tile(seq, ATTN_Q_TILE)))
            mixer_weights = (sb_w_o[j].astype(BF16),)
        h = _tail_ffn(a.reshape(rows, d), h2d, mixer_weights,
                      (vec(ln_mix_g[i]), vec(ln_mix_b[i])),
                      ffn_w_gu[i].astype(BF16), ffn_w_down[i].astype(BF16),
                      (vec(ln_ffn_g[i]), vec(ln_ffn_b[i])),
                      alpha=alpha, row_tile=ffn_tile).reshape(bsz, seq, d)
    return h
```

```python
import functools
import math

import jax
import jax.numpy as jnp
import numpy as np
from jax import lax
from jax.experimental import pallas as pl
from jax.experimental.pallas import tpu as pltpu

F32 = jnp.float32
BF16 = jnp.bfloat16

V7X_LANES = 128
V7X_SUBLANES = 8
V7X_VMEM_LIMIT_BYTES = 56 * 1024 * 1024

SSM_GROUP = 16
HEAD_DIM = 64
LN_EPS = 1e-5
LAM_RE_MAX = -1e-4

S5_TIME_TILE = 512
S5_SUB_STEPS = 128
ROW_TILE = 512
FFN_ROW_TILE = 512
ATTN_Q_TILE = 512
ATTN_K_TILE = 256


def _params(*sem):
    return pltpu.CompilerParams(dimension_semantics=sem,
                                vmem_limit_bytes=V7X_VMEM_LIMIT_BYTES)


def _resident(shape):
    zeros = (0,) * len(shape)
    return pl.BlockSpec(shape, lambda *_: zeros, pipeline_mode=pl.Buffered(1))


def _sigmoid(x):
    return 1.0 / (1.0 + jnp.exp(-x))


def _gelu_tanh(x):
    c = math.sqrt(2.0 / math.pi)
    return 0.5 * x * (1.0 + jnp.tanh(c * (x + 0.044715 * (x * x * x))))


def _layer_norm(r, g, b):
    mu = jnp.mean(r, axis=-1, keepdims=True)
    d = r - mu
    var = jnp.mean(d * d, axis=-1, keepdims=True)
    return d * lax.rsqrt(var + LN_EPS) * g + b


def _s5_kernel(x_ref, bm_ref, cm_ref, are_ref, aim_ref, d_ref, h_ref,
               state_ref, buf_a, buf_b, xs_ref, ys_ref, *, sub_steps, n_sub, batch):
    half = buf_a.shape[1] // 2
    nvr = half // V7X_LANES
    sub_rows = sub_steps * batch
    bufs = (buf_a, buf_b)

    @pl.when(pl.program_id(1) == 0)
    def _():
        state_ref[...] = jnp.zeros_like(state_ref)

    def lanes(p, imag):
        lo = (half if imag else 0) + p * V7X_LANES
        return slice(lo, lo + V7X_LANES)

    def sub(k):
        return slice(k * sub_rows, (k + 1) * sub_rows)

    def steps(k):
        return slice(k * sub_steps, (k + 1) * sub_steps)

    def project_in(k, buf):
        for b in range(batch):
            xs_ref[pl.ds(k * sub_rows + b, sub_steps, stride=batch), :] = x_ref[b, steps(k), :]
        buf[...] = jnp.dot(xs_ref[sub(k), :].astype(BF16), bm_ref[0],
                           preferred_element_type=F32)

    def project_out(k, buf):
        y = jnp.dot(buf[...].astype(BF16), cm_ref[0], preferred_element_type=F32)
        y = y + d_ref[...] * xs_ref[sub(k), :]
        ys_ref[...] = _gelu_tanh(y)
        for b in range(batch):
            h_ref[b, steps(k), :] = ys_ref[pl.ds(b, sub_steps, stride=batch), :].astype(
                h_ref.dtype)

    a_re = [are_ref[0, :, lanes(p, False)] for p in range(nvr)]
    a_im = [aim_ref[0, :, lanes(p, False)] for p in range(nvr)]
    s_re = [state_ref[:, lanes(p, False)] for p in range(nvr)]
    s_im = [state_ref[:, lanes(p, True)] for p in range(nvr)]

    def recur(buf):
        for t in range(sub_steps):
            rows = slice(t * batch, (t + 1) * batch)
            for p in range(nvr):
                v_re = a_re[p] * s_re[p] - a_im[p] * s_im[p] + buf[rows, lanes(p, False)]
                v_im = a_re[p] * s_im[p] + a_im[p] * s_re[p] + buf[rows, lanes(p, True)]
                buf[rows, lanes(p, False)] = v_re
                buf[rows, lanes(p, True)] = v_im
                s_re[p], s_im[p] = v_re, v_im

    project_in(0, bufs[0])
    for k in range(n_sub):
        cur, other = bufs[k % 2], bufs[(k + 1) % 2]
        if k >= 1:
            project_out(k - 1, other)
        if k + 1 < n_sub:
            project_in(k + 1, other)
        recur(cur)
    project_out(n_sub - 1, bufs[(n_sub - 1) % 2])

    for p in range(nvr):
        state_ref[:, lanes(p, False)] = s_re[p]
        state_ref[:, lanes(p, True)] = s_im[p]


def _s5_operators(a_re, a_im, log_dt, b_re, b_im, c_re, c_im):
    g, p = a_re.shape
    h = b_re.shape[-1]
    gl = V7X_LANES // h
    nj = g // gl
    dt = jnp.exp(log_dt.astype(F32))[:, None]
    lam_re = jnp.minimum(a_re.astype(F32), LAM_RE_MAX)
    lam_im = a_im.astype(F32)
    mag = jnp.exp(lam_re * dt)
    lbar_re = mag * jnp.cos(lam_im * dt)
    lbar_im = mag * jnp.sin(lam_im * dt)
    den = lam_re * lam_re + lam_im * lam_im
    num_re = lbar_re - 1.0
    num_im = lbar_im
    coef_re = (num_re * lam_re + num_im * lam_im) / den
    coef_im = (num_im * lam_re - num_re * lam_im) / den
    cr, ci = coef_re[..., None], coef_im[..., None]
    br, bi = b_re.astype(F32), b_im.astype(F32)
    bb_re = cr * br - ci * bi
    bb_im = cr * bi + ci * br
    eye = jnp.eye(gl, dtype=F32)

    def b_blocks(bb):
        t = bb.transpose(0, 2, 1).reshape(nj, gl, h, p)
        return jnp.einsum('jahp,ab->jahbp', t, eye).reshape(nj, gl * h, gl * p)

    def c_blocks(c):
        t = c.astype(F32).reshape(nj, gl, h, p)
        return jnp.einsum('jahp,ab->japbh', t, eye).reshape(nj, gl * p, gl * h)

    bm = jnp.concatenate([b_blocks(bb_re), b_blocks(bb_im)], axis=2).astype(BF16)
    cm = jnp.concatenate([c_blocks(c_re), -c_blocks(c_im)], axis=1).astype(BF16)
    are = jnp.broadcast_to(lbar_re.reshape(nj, 1, gl * p), (nj, V7X_SUBLANES, gl * p))
    aim = jnp.broadcast_to(lbar_im.reshape(nj, 1, gl * p), (nj, V7X_SUBLANES, gl * p))
    return bm, cm, are, aim


def _s5_scan(x, bm, cm, are, aim, d_skip, *, time_tile, sub_steps):
    batch, seq, d = x.shape
    nj = d // V7X_LANES
    nstate = bm.shape[2]
    sub_rows = sub_steps * batch
    kern = functools.partial(_s5_kernel, sub_steps=sub_steps, n_sub=time_tile // sub_steps,
                             batch=batch)
    io_block = pl.BlockSpec((batch, time_tile, V7X_LANES), lambda j, c: (0, c, j))
    return pl.pallas_call(
        kern,
        out_shape=jax.ShapeDtypeStruct((batch, seq, d), BF16),
        grid=(nj, seq // time_tile),
        in_specs=[
            io_block,
            pl.BlockSpec((1, V7X_LANES, nstate), lambda j, c: (j, 0, 0)),
            pl.BlockSpec((1, nstate, V7X_LANES), lambda j, c: (j, 0, 0)),
            pl.BlockSpec((1, V7X_SUBLANES, nstate // 2), lambda j, c: (j, 0, 0)),
            pl.BlockSpec((1, V7X_SUBLANES, nstate // 2), lambda j, c: (j, 0, 0)),
            pl.BlockSpec((1, V7X_LANES), lambda j, c: (0, j)),
        ],
        out_specs=io_block,
        scratch_shapes=[pltpu.VMEM((batch, nstate), F32),
                        pltpu.VMEM((sub_rows, nstate), F32),
                        pltpu.VMEM((sub_rows, nstate), F32),
                        pltpu.VMEM((time_tile * batch, V7X_LANES), F32),
                        pltpu.VMEM((sub_rows, V7X_LANES), F32)],
        compiler_params=_params("arbitrary", "arbitrary"),
        name="s5_scan",
    )(x, bm, cm, are, aim, d_skip)


def _ffn_ln_value(x, wgu_ref, wdown_ref, g_ref, b_ref, alpha):
    f = wdown_ref.shape[0]
    gu = jnp.dot(x.astype(BF16), wgu_ref[...], preferred_element_type=F32)
    gate = gu[:, :f]
    act = gate * _sigmoid(gate) * gu[:, f:]
    y = jnp.dot(act.astype(BF16), wdown_ref[...], preferred_element_type=F32)
    return _layer_norm(alpha * x + y, g_ref[...], b_ref[...])


def _tail_ffn_kernel(a_ref, x_ref, *refs, alpha, glu):
    if glu:
        wglu_ref, refs = refs[0], refs[1:]
        d = x_ref.shape[-1]
        vg = jnp.dot(a_ref[...], wglu_ref[...], preferred_element_type=F32)
        a = (vg[:, :d] * _sigmoid(vg[:, d:])).astype(BF16)
    else:
        a = a_ref[...]
    wout_ref, g1_ref, b1_ref, wgu_ref, wdown_ref, g2_ref, b2_ref, o_ref = refs
    mix = jnp.dot(a, wout_ref[...], preferred_element_type=F32)
    h = _layer_norm(alpha * x_ref[...] + mix, g1_ref[...], b1_ref[...])
    o_ref[...] = _ffn_ln_value(h, wgu_ref, wdown_ref, g2_ref, b2_ref, alpha)


def _tail_ffn(a2d, x2d, mixer_weights, ln1, w_gu, w_down, ln2, *, alpha, row_tile):
    rows, d = x2d.shape
    consts = (*mixer_weights, *ln1, w_gu, w_down, *ln2)
    kern = functools.partial(_tail_ffn_kernel, alpha=alpha, glu=len(mixer_weights) == 2)
    return pl.pallas_call(
        kern,
        out_shape=jax.ShapeDtypeStruct((rows, d), F32),
        grid=(rows // row_tile,),
        in_specs=[pl.BlockSpec((row_tile, d), lambda i: (i, 0)),
                  pl.BlockSpec((row_tile, d), lambda i: (i, 0))]
                 + [_resident(c.shape) for c in consts],
        out_specs=pl.BlockSpec((row_tile, d), lambda i: (i, 0)),
        compiler_params=_params("arbitrary"),
        name="tail_ffn",
    )(a2d, x2d, *consts)


def _qkv_kernel(x_ref, w_ref, o_ref, *, d, scale):
    acc = jnp.dot(x_ref[...].astype(BF16), w_ref[...], preferred_element_type=F32)
    o_ref[:, :d] = (acc[:, :d] * scale).astype(o_ref.dtype)
    o_ref[:, d:] = acc[:, d:].astype(o_ref.dtype)


def _qkv_proj(x2d, w_qkv, *, row_tile):
    rows, d = x2d.shape
    kern = functools.partial(_qkv_kernel, d=d, scale=HEAD_DIM ** -0.5)
    return pl.pallas_call(
        kern,
        out_shape=jax.ShapeDtypeStruct((rows, 3 * d), BF16),
        grid=(rows // row_tile,),
        in_specs=[pl.BlockSpec((row_tile, d), lambda i: (i, 0)),
                  _resident(w_qkv.shape)],
        out_specs=pl.BlockSpec((row_tile, 3 * d), lambda i: (i, 0)),
        compiler_params=_params("arbitrary"),
        name="qkv_proj",
    )(x2d, w_qkv)


ATTN_STAGES = 3
ATTN_UNROLL = 8
ATTN_SLOTS = 4
ATTN_MASKED_OUT = -1e30
NEG_LOG2E = -1.4426950408889634
TAB_QROW, TAB_KROW, TAB_MASK = range(3)


def _attn_schedule(seq, q_tile, k_tile):
    diag = q_tile // k_tile
    per_body = ATTN_UNROLL // 2
    masked, below = [], []
    for i in range(seq // q_tile):
        n_kv = (i + 1) * diag
        for jj in range(n_kv):
            tile = (i * q_tile, (n_kv - 1 - jj) * k_tile, min(jj, diag))
            (masked if jj < diag else below).append(tile)
    spill = len(below) % per_body
    masked, below = masked + below[:spill], below[spill:]
    idle = (0, 0, diag + 1)
    masked += [idle] * (-len(masked) % per_body)
    lead = (ATTN_STAGES - 1) // 2

    def dead_rows(tiles, pos):
        kinds = {t[2] for t in tiles[pos::per_body]} - {diag + 1}
        return min(((diag - 1 - kind) * k_tile if kind < diag else 0 for kind in kinds),
                   default=0)

    cols, phases = [], []
    for tiles, use_mask in ((masked, True), (below, False)):
        if tiles:
            skip = tuple(dead_rows(tiles, pos) if use_mask else 0 for pos in range(per_body))
            phases.append((len(cols) + lead, len(tiles) // per_body, use_mask, skip))
            cols += [idle] * lead + tiles
    return np.asarray(cols, np.int32).T, tuple(phases)


def _attn_masks(q_tile, k_tile):
    diag = q_tile // k_tile
    r = lax.broadcasted_iota(jnp.int32, (q_tile, k_tile), 0)
    c = lax.broadcasted_iota(jnp.int32, (q_tile, k_tile), 1)
    allowed = [((diag - 1 - t) * k_tile + c) < r for t in range(diag)]
    allowed += [jnp.ones((q_tile, k_tile), bool), jnp.zeros((q_tile, k_tile), bool)]
    return jnp.where(jnp.stack(allowed), 0.0, ATTN_MASKED_OUT).astype(F32)


def _attn_kernel(tab_ref, q_ref, k_ref, v_ref, u_ref, mask_ref, o_ref,
                 qh_ref, z_buf, sp_buf, w_buf, acc_a, acc_b, tot_a, tot_b,
                 *, q_tile, k_tile, phases):
    q2 = q_ref[...]
    lane = lax.broadcasted_iota(jnp.int32, q2.shape, 1)
    zero = jnp.zeros_like(q2)
    qh_ref[0] = jnp.where(lane < HEAD_DIM, q2, zero)
    qh_ref[1] = jnp.where(lane >= HEAD_DIM, q2, zero)
    for buf in (acc_a, acc_b, tot_a, tot_b):
        buf[...] = jnp.zeros_like(buf)
    umat = u_ref[...]
    nt = (((1,), (1,)), ((), ()))
    acc_refs = (acc_a, acc_b)
    tot_refs = (tot_a, tot_b)

    def k_rows(e):
        return pl.ds(pl.multiple_of(tab_ref[TAB_KROW, e], k_tile), k_tile)

    def q_rows(e, r0):
        start = pl.multiple_of(tab_ref[TAB_QROW, e] + r0, k_tile)
        return pl.ds(start, q_tile - r0)

    def stage_scores(e, hd, slot, r0, use_mask):
        z = lax.dot_general(qh_ref[hd, q_rows(e, r0), :], k_ref[k_rows(e), :], nt,
                            preferred_element_type=F32)
        if use_mask:
            z = z + mask_ref[tab_ref[TAB_MASK, e], r0:, :]
        z_buf[slot, r0:, :] = z
        sp = jnp.maximum(z, 0.0) + jnp.log(1.0 + jnp.exp2(jnp.abs(z) * NEG_LOG2E))
        sp_buf[slot, r0:, :] = sp.astype(BF16)

    def stage_weights(e, hd, slot, r0):
        rows = q_rows(e, r0)
        tot = tot_refs[hd][rows, :]
        suffix = jnp.dot(sp_buf[slot, r0:, :], umat, preferred_element_type=F32)
        w_buf[slot, r0:, :] = jnp.exp(z_buf[slot, r0:, :] - suffix - tot).astype(BF16)
        tot_refs[hd][rows, :] = tot + suffix[:, 0:1]

    def stage_pv(e, hd, slot, r0):
        rows = q_rows(e, r0)
        acc_refs[hd][rows, :] += jnp.dot(w_buf[slot, r0:, :], v_ref[k_rows(e), :],
                                         preferred_element_type=F32)

    def run_phase(slot_shift, first_col, n_body, use_mask, skip):
        stages = (functools.partial(stage_scores, use_mask=use_mask), stage_weights, stage_pv)
        assert len(stages) == ATTN_STAGES

        def slot_of(item):
            return (item + slot_shift) % ATTN_SLOTS

        def run(stage, base, item):
            stage(base + item // 2, item % 2, slot_of(item), skip[(item // 2) % len(skip)])

        for item in range(1 - ATTN_STAGES, 0):
            w_buf[slot_of(item)] = jnp.zeros((q_tile, k_tile), BF16)
        for item in range(2 - ATTN_STAGES, 0):
            z_buf[slot_of(item)] = jnp.full((q_tile, k_tile), ATTN_MASKED_OUT, F32)
            sp_buf[slot_of(item)] = jnp.zeros((q_tile, k_tile), BF16)

        def body(m, carry):
            base = (ATTN_UNROLL // 2) * m + first_col
            for u in range(ATTN_UNROLL):
                for k, stage in enumerate(stages):
                    run(stage, base, u - k)
            return carry

        lax.fori_loop(0, n_body, body, 0)
        for t in range(ATTN_STAGES - 1):
            for k in range(t + 1, ATTN_STAGES):
                run(stages[k], first_col, ATTN_UNROLL * n_body + t - k)

    for p, phase in enumerate(phases):
        run_phase((ATTN_STAGES - 1) * (p % 2), *phase)
    o_ref[...] = jnp.where(lane < HEAD_DIM, acc_a[...], acc_b[...]).astype(o_ref.dtype)


def _stick_breaking(qkv, *, q_tile, k_tile):
    bsz, seq, d3 = qkv.shape
    d = d3 // 3
    pairs = d // V7X_LANES
    rr = lax.broadcasted_iota(jnp.int32, (k_tile, k_tile), 0)
    cc = lax.broadcasted_iota(jnp.int32, (k_tile, k_tile), 1)
    umat = (rr >= cc).astype(BF16)
    table, phases = _attn_schedule(seq, q_tile, k_tile)
    mask = _attn_masks(q_tile, k_tile)
    kern = functools.partial(_attn_kernel, q_tile=q_tile, k_tile=k_tile, phases=phases)
    seq_block = lambda off: pl.BlockSpec((None, seq, V7X_LANES),
                                         lambda b, p, tab: (b, 0, off * pairs + p))
    grid_spec = pltpu.PrefetchScalarGridSpec(
        num_scalar_prefetch=1,
        grid=(bsz, pairs),
        in_specs=[seq_block(0), seq_block(1), seq_block(2),
                  _resident(umat.shape), _resident(mask.shape)],
        out_specs=seq_block(0),
        scratch_shapes=[
            pltpu.VMEM((2, seq, V7X_LANES), BF16),
            pltpu.VMEM((ATTN_SLOTS, q_tile, k_tile), F32),
            pltpu.VMEM((ATTN_SLOTS, q_tile, k_tile), BF16),
            pltpu.VMEM((ATTN_SLOTS, q_tile, k_tile), BF16),
            pltpu.VMEM((seq, V7X_LANES), F32),
            pltpu.VMEM((seq, V7X_LANES), F32),
            pltpu.VMEM((seq, 1), F32),
            pltpu.VMEM((seq, 1), F32),
        ])
    return pl.pallas_call(
        kern,
        out_shape=jax.ShapeDtypeStruct((bsz, seq, d), BF16),
        grid_spec=grid_spec,
        compiler_params=_params("arbitrary", "arbitrary"),
        name="stick_breaking",
    )(jnp.asarray(table), qkv, qkv, qkv, umat, mask)


def _tile(n, pref):
    return pref if n % pref == 0 else n


def kernel(x, ssm_a_re, ssm_a_im, ssm_log_dt, ssm_b_re, ssm_b_im, ssm_c_re, ssm_c_im, ssm_d,
           ssm_w_glu, ssm_w_out, sb_w_qkv, sb_w_o, ffn_w_gu, ffn_w_down,
           ln_mix_g, ln_mix_b, ln_ffn_g, ln_ffn_b):
    bsz, seq, d = x.shape
    depth = ffn_w_gu.shape[0]
    assert bsz == V7X_SUBLANES and d % V7X_LANES == 0
    alpha = (2 * depth) ** 0.25
    rows = bsz * seq
    ffn_tile = _tile(rows, FFN_ROW_TILE)
    vec = lambda a: a.reshape(1, d).astype(F32)

    h = x.astype(F32)
    for i in range(depth):
        j = i // 2
        h2d = h.reshape(rows, d)
        if i % 2 == 0:
            bm, cm, are, aim = _s5_operators(ssm_a_re[j], ssm_a_im[j], ssm_log_dt[j],
                                             ssm_b_re[j], ssm_b_im[j], ssm_c_re[j], ssm_c_im[j])
            time_tile = _tile(seq, S5_TIME_TILE)
            a = _s5_scan(h, bm, cm, are, aim, vec(ssm_d[j]), time_tile=time_tile,
                         sub_steps=_tile(time_tile, S5_SUB_STEPS))
            mixer_weights = (ssm_w_glu[j].astype(BF16), ssm_w_out[j].astype(BF16))
        else:
            qkv = _qkv_proj(h2d, sb_w_qkv[j].astype(BF16), row_tile=_tile(rows, ROW_TILE))
            a = _stick_breaking(qkv.reshape(bsz, seq, 3 * d),
                                q_tile=_tile(seq, ATTN_Q_TILE),
                                k_tile=min(ATTN_K_TILE, _tile(seq, ATTN_Q_TILE)))
            mixer_weights = (sb_w_o[j].astype(BF16),)
        h = _tail_ffn(a.reshape(rows, d), h2d, mixer_weights,
                      (vec(ln_mix_g[i]), vec(ln_mix_b[i])),
                      ffn_w_gu[i].astype(BF16), ffn_w_down[i].astype(BF16),
                      (vec(ln_ffn_g[i]), vec(ln_ffn_b[i])),
                      alpha=alpha, row_tile=ffn_tile).reshape(bsz, seq, d)
    return h
```
